```python
import jax, jax.numpy as jnp
from jax import lax
import numpy as np

D_MODEL = 1024
BATCH = 8
SEQ = 2048
DEPTH = 2

HEAD_DIM = 64
GROUP_HEADS = 4
GROUP_WIDTH = GROUP_HEADS * HEAD_DIM
N_GROUPS = 4
MIX_WIDTH = N_GROUPS * GROUP_WIDTH
MLA_Q_RANK = 256
MLA_KV_RANK = 128
MLA_NOPE = 64
MLA_ROPE = 32
MLA_QK = MLA_NOPE + MLA_ROPE
MLA_V = 64
ROPE_BASE = 10000.0
Q_BLOCK = 128
MLSTM_CONV = 4
MLSTM_CHUNK = 64
FGATE_BIAS_LO = 3.0
FGATE_BIAS_HI = 6.0
SPATIAL_CHUNK = 128
CONV_WIDTH = 31
PEER_HEADS = 8
PEER_KEYS = 128
PEER_EXPERTS = PEER_KEYS * PEER_KEYS
PEER_TOPK = 16
PEER_QDIM = 128
PEER_HALF = PEER_QDIM // 2
PEER_TOKEN_BLOCK = 128
EPS = 1e-6

SPLIT_SIZES = (MLA_Q_RANK, MLA_KV_RANK, MLA_ROPE,
               GROUP_WIDTH, GROUP_WIDTH, GROUP_WIDTH, GROUP_HEADS, GROUP_HEADS,
               GROUP_WIDTH, GROUP_WIDTH,
               GROUP_WIDTH, GROUP_WIDTH)
PROJ_WIDTH = sum(SPLIT_SIZES)

kernel_name = 'hybrid_mla_mlstm_gmlp_conv_peer'


def _split_points():
    pts, acc = [], 0
    for s in SPLIT_SIZES[:-1]:
        acc += s
        pts.append(acc)
    return pts


def rmsnorm(x, g):
    xf = x.astype(jnp.float32)
    y = xf * lax.rsqrt(jnp.mean(xf * xf, axis=-1, keepdims=True) + EPS)
    return (y * g.astype(jnp.float32)).astype(x.dtype)


def layernorm(x, g, b):
    xf = x.astype(jnp.float32)
    mu = jnp.mean(xf, axis=-1, keepdims=True)
    var = jnp.mean(jnp.square(xf - mu), axis=-1, keepdims=True)
    y = (xf - mu) * lax.rsqrt(var + EPS) * g.astype(jnp.float32) + b.astype(jnp.float32)
    return y.astype(x.dtype)


def apply_rope(x, cos, sin):
    x1, x2 = x[..., :MLA_ROPE // 2], x[..., MLA_ROPE // 2:]
    return jnp.concatenate([x1 * cos - x2 * sin, x1 * sin + x2 * cos], axis=-1)


def causal_dwconv(x, w, b):
    k = w.shape[0]
    y = lax.conv_general_dilated(x, w[:, None, :].astype(x.dtype), window_strides=(1,),
                                 padding=((k - 1, 0),),
                                 dimension_numbers=('NWC', 'WIO', 'NWC'),
                                 feature_group_count=x.shape[-1])
    return y + b.astype(x.dtype)


def mla_mixer(cq, ckv, kr, cos, sin, cq_g, ckv_g, w_uq, w_ukv, qn_g, kn_g):
    bsz, s, _ = cq.shape
    q = (rmsnorm(cq, cq_g) @ w_uq).reshape(bsz, s, GROUP_HEADS, MLA_QK)
    kv = (rmsnorm(ckv, ckv_g) @ w_ukv).reshape(bsz, s, GROUP_HEADS, MLA_NOPE + MLA_V)
    k_nope, v = kv[..., :MLA_NOPE], kv[..., MLA_NOPE:]
    k_rope = jnp.broadcast_to(kr[:, :, None, :], (bsz, s, GROUP_HEADS, MLA_ROPE))
    k = jnp.concatenate([k_nope, k_rope], axis=-1)
    q = rmsnorm(q, qn_g)
    k = rmsnorm(k, kn_g)
    q = jnp.concatenate([q[..., :MLA_NOPE], apply_rope(q[..., MLA_NOPE:], cos, sin)], axis=-1)
    k = jnp.concatenate([k[..., :MLA_NOPE], apply_rope(k[..., MLA_NOPE:], cos, sin)], axis=-1)
    q = q.transpose(0, 2, 1, 3)
    k = k.transpose(0, 2, 1, 3)
    v = v.transpose(0, 2, 1, 3)
    nb = s // Q_BLOCK
    qb = jnp.moveaxis(q.reshape(bsz, GROUP_HEADS, nb, Q_BLOCK, MLA_QK), 2, 0)
    key_pos = jnp.arange(s)
    scale = MLA_QK ** -0.5

    def block(args):
        qi, start = args
        sc = jnp.einsum('bhqd,bhkd->bhqk', qi, k).astype(jnp.float32) * scale
        qpos = start + jnp.arange(Q_BLOCK)
        sc = jnp.where(key_pos[None, :] <= qpos[:, None], sc, -jnp.inf)
        p = jax.nn.softmax(sc, axis=-1).astype(v.dtype)
        return jnp.einsum('bhqk,bhkd->bhqd', p, v)

    out = lax.map(block, (qb, jnp.arange(nb) * Q_BLOCK))
    return out.transpose(1, 0, 3, 2, 4).reshape(bsz, s, GROUP_HEADS * MLA_V)


def mlstm_mixer(xqk, xv, xo, ig_pre, fg_pre, conv_w, conv_b, w_q, w_k, b_i, b_f, hn_g):
    bsz, s, _ = xqk.shape
    xc = jax.nn.silu(causal_dwconv(xqk, conv_w, conv_b)).reshape(bsz, s, GROUP_HEADS, HEAD_DIM)
    q = jnp.einsum('bshd,hde->bhse', xc, w_q).astype(jnp.float32)
    k = jnp.einsum('bshd,hde->bhse', xc, w_k).astype(jnp.float32) * (HEAD_DIM ** -0.5)
    v = xv.reshape(bsz, s, GROUP_HEADS, HEAD_DIM).transpose(0, 2, 1, 3).astype(jnp.float32)
    ig = (ig_pre + b_i).astype(jnp.float32).transpose(0, 2, 1)
    lf = jax.nn.log_sigmoid((fg_pre + b_f).astype(jnp.float32)).transpose(0, 2, 1)
    nc = s // MLSTM_CHUNK

    def chunks(t):
        return jnp.moveaxis(t.reshape(bsz, GROUP_HEADS, nc, MLSTM_CHUNK, *t.shape[3:]), 2, 0)

    tri = jnp.tril(jnp.ones((MLSTM_CHUNK, MLSTM_CHUNK), dtype=bool))

    def step(carry, inp):
        c_st, n_st, m_st = carry
        qc, kc, vc, ic, fc = inp
        bcum = jnp.cumsum(fc, axis=-1)
        dmat = jnp.where(tri, bcum[..., :, None] - bcum[..., None, :] + ic[..., None, :], -jnp.inf)
        m_inter = bcum + m_st[..., None]
        m_t = jnp.maximum(jnp.max(dmat, axis=-1), m_inter)
        w = jnp.exp(dmat - m_t[..., None]) * jnp.einsum('bhtd,bhsd->bhts', qc, kc)
        inter = jnp.exp(m_inter - m_t)
        num = jnp.einsum('bhts,bhsd->bhtd', w, vc) + inter[..., None] * jnp.einsum('bhvk,bhtk->bhtv', c_st, qc)
        den = jnp.sum(w, axis=-1) + inter * jnp.einsum('bhk,bhtk->bht', n_st, qc)
        h = num / jnp.maximum(jnp.abs(den), jnp.exp(-m_t))[..., None]
        b_last = bcum[..., -1]
        g = b_last[..., None] - bcum + ic
        m_new = jnp.maximum(b_last + m_st, jnp.max(g, axis=-1))
        decay = jnp.exp(b_last + m_st - m_new)
        wg = jnp.exp(g - m_new[..., None])
        c_new = decay[..., None, None] * c_st + jnp.einsum('bhs,bhsv,bhsk->bhvk', wg, vc, kc)
        n_new = decay[..., None] * n_st + jnp.einsum('bhs,bhsk->bhk', wg, kc)
        return (c_new, n_new, m_new), h

    init = (jnp.zeros((bsz, GROUP_HEADS, HEAD_DIM, HEAD_DIM), jnp.float32),
            jnp.zeros((bsz, GROUP_HEADS, HEAD_DIM), jnp.float32),
            jnp.zeros((bsz, GROUP_HEADS), jnp.float32))
    _, hs = lax.scan(step, init, (chunks(q), chunks(k), chunks(v), chunks(ig), chunks(lf)))
    h = jnp.moveaxis(hs, 0, 2).reshape(bsz, GROUP_HEADS, s, HEAD_DIM).transpose(0, 2, 1, 3)
    h = rmsnorm(h, hn_g).reshape(bsz, s, GROUP_WIDTH)
    return (jax.nn.sigmoid(xo.astype(jnp.float32)) * h).astype(xqk.dtype)


def spatial_gating_mixer(xu, xv, ln_g, ln_b, w_s, b_s):
    bsz, s, _ = xu.shape
    u = jax.nn.gelu(xu)
    v = layernorm(jax.nn.gelu(xv), ln_g, ln_b)
    nc = s // SPATIAL_CHUNK
    v = v.reshape(bsz, nc, SPATIAL_CHUNK, GROUP_HEADS, HEAD_DIM)
    tri = jnp.tril(jnp.ones((SPATIAL_CHUNK, SPATIAL_CHUNK), dtype=bool))
    ws = jnp.where(tri[None], w_s, 0.0).astype(v.dtype)
    sg = jnp.einsum('gts,bcsgd->bctgd', ws, v) + b_s.T[None, None, :, :, None].astype(v.dtype)
    return u * sg.reshape(bsz, s, GROUP_WIDTH)


def conv_module_mixer(xa, xb, dw_w, dw_b, cn_g, cn_b):
    bsz, s, _ = xa.shape
    y = causal_dwconv(xa * jax.nn.sigmoid(xb), dw_w, dw_b)
    y = layernorm(y.reshape(bsz, s, GROUP_HEADS, HEAD_DIM), cn_g, cn_b)
    return jax.nn.silu(y).reshape(bsz, s, GROUP_WIDTH)


def peer_ffn(x, w_pq, sub_keys, u_emb, v_emb):
    bsz, s, d = x.shape
    t = bsz * s
    xt = x.reshape(t, d)
    q = (xt @ w_pq).reshape(t, PEER_HEADS, 2, PEER_HALF)
    sc = jnp.einsum('thpd,hpnd->thpn', q, sub_keys).astype(jnp.float32)
    s1, i1 = lax.top_k(sc[:, :, 0], PEER_TOPK)
    s2, i2 = lax.top_k(sc[:, :, 1], PEER_TOPK)
    cand = (s1[..., :, None] + s2[..., None, :]).reshape(t, PEER_HEADS, PEER_TOPK * PEER_TOPK)
    top_s, top_i = lax.top_k(cand, PEER_TOPK)
    e1 = jnp.take_along_axis(i1, top_i // PEER_TOPK, axis=-1)
    e2 = jnp.take_along_axis(i2, top_i % PEER_TOPK, axis=-1)
    experts = (e1 * PEER_KEYS + e2).reshape(t, PEER_HEADS * PEER_TOPK)
    gates = jax.nn.softmax(top_s, axis=-1).reshape(t, PEER_HEADS * PEER_TOPK)
    nb = t // PEER_TOKEN_BLOCK

    def block(args):
        xb, eb, gb = args
        a = jnp.einsum('tkd,td->tk', u_emb[eb], xb).astype(jnp.float32)
        act = (jax.nn.gelu(a) * gb).astype(xb.dtype)
        return jnp.einsum('tk,tkd->td', act, v_emb[eb])

    out = lax.map(block, (xt.reshape(nb, PEER_TOKEN_BLOCK, d),
                          experts.reshape(nb, PEER_TOKEN_BLOCK, PEER_HEADS * PEER_TOPK),
                          gates.reshape(nb, PEER_TOKEN_BLOCK, PEER_HEADS * PEER_TOPK)))
    return out.reshape(bsz, s, d)


def setup_inputs(seed: int = 0) -> dict:
    key = jax.random.key(seed)
    ks = jax.random.split(key, 40)

    def nrm(k, shape, scale):
        return jax.random.normal(k, shape, jnp.float32) * scale

    def gain(k, shape):
        return 1.0 + 0.05 * jax.random.normal(k, shape, jnp.float32)

    L = DEPTH
    x = jax.random.normal(ks[0], (BATCH, SEQ, D_MODEL), jnp.float32)
    offset = jax.random.randint(ks[1], (BATCH, 1), 0, 4096, dtype=jnp.int32)
    positions = (offset + jnp.arange(SEQ, dtype=jnp.int32)[None, :]).astype(jnp.int32)
    b_f = (jnp.linspace(FGATE_BIAS_LO, FGATE_BIAS_HI, GROUP_HEADS, dtype=jnp.float32)[None, :]
           + 0.1 * jax.random.normal(ks[16], (L, GROUP_HEADS), jnp.float32))
    return {
        'x': x,
        'positions': positions,
        'norm1_g': gain(ks[2], (L, D_MODEL)),
        'w_in': nrm(ks[3], (L, D_MODEL, PROJ_WIDTH), D_MODEL ** -0.5),
        'a_cq_g': gain(ks[4], (L, MLA_Q_RANK)),
        'a_ckv_g': gain(ks[5], (L, MLA_KV_RANK)),
        'a_w_uq': nrm(ks[6], (L, MLA_Q_RANK, GROUP_HEADS * MLA_QK), MLA_Q_RANK ** -0.5),
        'a_w_ukv': nrm(ks[7], (L, MLA_KV_RANK, GROUP_HEADS * (MLA_NOPE + MLA_V)), MLA_KV_RANK ** -0.5),
        'a_qn_g': gain(ks[8], (L, MLA_QK)),
        'a_kn_g': gain(ks[9], (L, MLA_QK)),
        'b_conv_w': nrm(ks[10], (L, MLSTM_CONV, GROUP_WIDTH), MLSTM_CONV ** -0.5),
        'b_conv_b': nrm(ks[11], (L, GROUP_WIDTH), 0.02),
        'b_w_q': nrm(ks[12], (L, GROUP_HEADS, HEAD_DIM, HEAD_DIM), HEAD_DIM ** -0.5),
        'b_w_k': nrm(ks[13], (L, GROUP_HEADS, HEAD_DIM, HEAD_DIM), HEAD_DIM ** -0.5),
        'b_b_i': nrm(ks[14], (L, GROUP_HEADS), 0.1),
        'b_b_f': b_f,
        'b_hn_g': gain(ks[15], (L, GROUP_HEADS, HEAD_DIM)),
        'c_ln_g': gain(ks[17], (L, GROUP_WIDTH)),
        'c_ln_b': nrm(ks[18], (L, GROUP_WIDTH), 0.02),
        'c_w_s': nrm(ks[19], (L, GROUP_HEADS, SPATIAL_CHUNK, SPATIAL_CHUNK), SPATIAL_CHUNK ** -0.5),
        'c_b_s': 1.0 + nrm(ks[20], (L, GROUP_HEADS, SPATIAL_CHUNK), 0.02),
        'd_dw_w': nrm(ks[21], (L, CONV_WIDTH, GROUP_WIDTH), CONV_WIDTH ** -0.5),
        'd_dw_b': nrm(ks[22], (L, GROUP_WIDTH), 0.02),
        'd_cn_g': gain(ks[23], (L, GROUP_HEADS, HEAD_DIM)),
        'd_cn_b': nrm(ks[24], (L, GROUP_HEADS, HEAD_DIM), 0.02),
        'w_out': nrm(ks[25], (L, MIX_WIDTH, D_MODEL), MIX_WIDTH ** -0.5),
        'norm2_g': gain(ks[26], (L, D_MODEL)),
        'p_w_q': nrm(ks[27], (L, D_MODEL, PEER_HEADS * PEER_QDIM), D_MODEL ** -0.5),
        'p_sub_keys': nrm(ks[28], (L, PEER_HEADS, 2, PEER_KEYS, PEER_HALF), PEER_HALF ** -0.5),
        'p_u': nrm(ks[29], (L, PEER_EXPERTS, D_MODEL), D_MODEL ** -0.5),
        'p_v': nrm(ks[30], (L, PEER_EXPERTS, D_MODEL), PEER_HEADS ** -0.5),
    }


def reference(x, positions, norm1_g, w_in, a_cq_g, a_ckv_g, a_w_uq, a_w_ukv, a_qn_g, a_kn_g,
              b_conv_w, b_conv_b, b_w_q, b_w_k, b_b_i, b_b_f, b_hn_g,
              c_ln_g, c_ln_b, c_w_s, c_b_s, d_dw_w, d_dw_b, d_cn_g, d_cn_b,
              w_out, norm2_g, p_w_q, p_sub_keys, p_u, p_v):
    inv_freq = ROPE_BASE ** (-jnp.arange(0, MLA_ROPE, 2, dtype=jnp.float32) / MLA_ROPE)
    ang = positions.astype(jnp.float32)[..., None] * inv_freq
    cos = jnp.cos(ang)[:, :, None, :].astype(x.dtype)
    sin = jnp.sin(ang)[:, :, None, :].astype(x.dtype)
    split_pts = _split_points()
    for l in range(DEPTH):
        hn = rmsnorm(x, norm1_g[l])
        proj = hn @ w_in[l]
        (a_cq, a_ckv, a_kr, b_qk, b_v, b_o, b_i, b_f,
         c_u, c_v, d_a, d_b) = jnp.split(proj, split_pts, axis=-1)
        out_a = mla_mixer(a_cq, a_ckv, a_kr, cos, sin, a_cq_g[l], a_ckv_g[l],
                          a_w_uq[l], a_w_ukv[l], a_qn_g[l], a_kn_g[l])
        out_b = mlstm_mixer(b_qk, b_v, b_o, b_i, b_f, b_conv_w[l], b_conv_b[l],
                            b_w_q[l], b_w_k[l], b_b_i[l], b_b_f[l], b_hn_g[l])
        out_c = spatial_gating_mixer(c_u, c_v, c_ln_g[l], c_ln_b[l], c_w_s[l], c_b_s[l])
        out_d = conv_module_mixer(d_a, d_b, d_dw_w[l], d_dw_b[l], d_cn_g[l], d_cn_b[l])
        mix = jnp.concatenate([out_a, out_b, out_c, out_d], axis=-1)
        x = x + mix @ w_out[l]
        x = x + peer_ffn(rmsnorm(x, norm2_g[l]), p_w_q[l], p_sub_keys[l], p_u[l], p_v[l])
    return x
```

```python
import functools

import jax
import jax.numpy as jnp
from jax import lax
from jax.experimental import pallas as pl
from jax.experimental.pallas import tpu as pltpu

F32 = jnp.float32
BF16 = jnp.bfloat16

D_MODEL = 1024
N_HEADS = 4
HEAD_DIM = 64
GROUP_W = N_HEADS * HEAD_DIM
Q_RANK, KV_RANK = 256, 128
NOPE, ROPE = 64, 32
QK_DIM = NOPE + ROPE
ROPE_BASE = 10000.0
LANE = 128
CONV_B = 4
CONV_D = 31
SP_CHUNK = 128
P_HEADS = 8
P_KEYS = 128
P_TOPK = 16
P_HALF = 64
N_EXPERTS = P_KEYS * P_KEYS
EPS = 1e-6
NEG = -1e30
VMEM_LIMIT = 56 * 1024 * 1024

NT = (((1,), (1,)), ((), ()))


def _cparams(sem):
    return pltpu.CompilerParams(dimension_semantics=sem, vmem_limit_bytes=VMEM_LIMIT)


def _full(shape):
    n = len(shape)
    return pl.BlockSpec(shape, lambda *_: (0,) * n)


def _split3(a):
    hi = a.astype(BF16)
    r1 = a - hi.astype(F32)
    mid = r1.astype(BF16)
    lo = (r1 - mid.astype(F32)).astype(BF16)
    return hi, mid, lo


def _dot_exact_rhs(a, m_bf16):
    hi, mid, lo = _split3(a)
    d = lambda t: jnp.dot(t, m_bf16, preferred_element_type=F32)
    return d(hi) + d(mid) + d(lo)


def _dot_exact_lhs(m_bf16, a):
    hi, mid, lo = _split3(a)
    d = lambda t: jnp.dot(m_bf16, t, preferred_element_type=F32)
    return d(hi) + d(mid) + d(lo)


def _sigmoid(x):
    return 1.0 / (1.0 + jnp.exp(-x))


def _gelu(x):
    return 0.5 * x * (1.0 + jnp.tanh(0.7978845608028654 * (x + 0.044715 * (x * x * x))))


PA_W, PB_W, PG_W, PC_W, PD_W = 512, 1024, 256, 512, 512
IN_COLS = (0, PA_W, PA_W + PB_W, PA_W + PB_W + PG_W, PA_W + PB_W + PG_W + PC_W,
           PA_W + PB_W + PG_W + PC_W + PD_W)


def _inproj_kernel(x_ref, g_ref, w_ref, pa_ref, pb_ref, pg_ref, pc_ref, pd_ref):
    x = x_ref[...]
    hn = (x * lax.rsqrt(jnp.mean(x * x, axis=-1, keepdims=True) + EPS) * g_ref[...]).astype(BF16)
    outs = (pa_ref, pb_ref, pg_ref, pc_ref, pd_ref)
    for k, o in enumerate(outs):
        y = jnp.dot(hn, w_ref[:, IN_COLS[k]:IN_COLS[k + 1]], preferred_element_type=F32)
        o[...] = y.astype(o.dtype)


def _inproj(x, g, w_all, tm=512):
    t = x.shape[0]
    widths = (PA_W, PB_W, PG_W, PC_W, PD_W)
    dts = (BF16, BF16, F32, BF16, BF16)
    return pl.pallas_call(
        _inproj_kernel,
        grid=(t // tm,),
        in_specs=[pl.BlockSpec((tm, D_MODEL), lambda i: (i, 0)),
                  _full((1, D_MODEL)), _full((D_MODEL, IN_COLS[-1]))],
        out_specs=[pl.BlockSpec((tm, w), lambda i: (i, 0)) for w in widths],
        out_shape=[jax.ShapeDtypeStruct((t, w), d) for w, d in zip(widths, dts)],
        compiler_params=_cparams(("parallel",)),
        name="inproj",
    )(x, g, w_all)


MLA_TQ = 256
MLA_PRO = 512


def _mla_kernel(pa_ref, pos_ref, invf_ref, cqg_ref, ckvg_ref, wuq_ref, wk_ref, wv_ref,
                qg_ref, kg_ref, o_ref, q_s, k_s, v_s, *, seq):
    i = pl.program_id(1)
    tq = MLA_TQ

    @pl.when(i == 0)
    def _prologue():
        def chunk(c, carry):
            r0 = pl.multiple_of(c * MLA_PRO, MLA_PRO)
            pa = pa_ref[0, pl.ds(r0, MLA_PRO), :]
            cq = pa[:, 0:Q_RANK].astype(F32)
            ckv = pa[:, Q_RANK:Q_RANK + KV_RANK].astype(F32)
            krp = pa[:, Q_RANK + KV_RANK:].astype(F32)
            cqn = (cq * lax.rsqrt(jnp.mean(cq * cq, -1, keepdims=True) + EPS) * cqg_ref[...]).astype(BF16)
            ckvn = (ckv * lax.rsqrt(jnp.mean(ckv * ckv, -1, keepdims=True) + EPS) * ckvg_ref[...]).astype(BF16)
            q = jnp.dot(cqn, wuq_ref[...], preferred_element_type=F32)
            kn = jnp.dot(ckvn, wk_ref[...], preferred_element_type=F32)
            v = jnp.dot(ckvn, wv_ref[...], preferred_element_type=F32)
            pos = pos_ref[0, pl.ds(r0, MLA_PRO), :].astype(F32)
            ang = pos * invf_ref[...]
            lane = lax.broadcasted_iota(jnp.int32, (MLA_PRO, LANE), 1)
            cosv = jnp.cos(ang)
            sinv = jnp.sin(ang)
            half = ROPE // 2
            sin_hi = jnp.where((lane >= NOPE + half) & (lane < QK_DIM), sinv, 0.0)
            sin_lo = jnp.where((lane >= NOPE) & (lane < NOPE + half), -sinv, 0.0)
            kr_sh = pltpu.roll(krp, NOPE, 1)

            def norm_rope(t, g):
                t = t * lax.rsqrt(jnp.sum(t * t, -1, keepdims=True) * (1.0 / QK_DIM) + EPS) * g
                return (t * cosv + pltpu.roll(t, half, 1) * sin_hi
                        + pltpu.roll(t, LANE - half, 1) * sin_lo)

            for h in range(N_HEADS):
                sl = slice(h * LANE, (h + 1) * LANE)
                q_s[pl.ds(r0, MLA_PRO), sl] = norm_rope(q[:, sl], qg_ref[...]).astype(BF16)
                k_s[pl.ds(r0, MLA_PRO), sl] = norm_rope(kn[:, sl] + kr_sh, kg_ref[...]).astype(BF16)
                v_s[pl.ds(r0, MLA_PRO), sl] = jnp.where(lane == HEAD_DIM, 1.0, v[:, sl]).astype(BF16)
            return carry

        lax.fori_loop(0, seq // MLA_PRO, chunk, 0)

    q0 = pl.multiple_of(i * tq, tq)
    row = lax.broadcasted_iota(jnp.int32, (tq, tq), 0) + i * tq
    col = lax.broadcasted_iota(jnp.int32, (tq, tq), 1)
    for h in range(N_HEADS):
        sl = slice(h * LANE, (h + 1) * LANE)
        qh = q_s[pl.ds(q0, tq), sl]

        def body(j, carry, sl=sl, qh=qh):
            m, acc = carry
            k0 = pl.multiple_of(j * tq, tq)
            kj = k_s[pl.ds(k0, tq), sl]
            vj = v_s[pl.ds(k0, tq), sl]
            s = lax.dot_general(qh, kj, NT, preferred_element_type=F32)
            s = jnp.where(col + j * tq <= row, s, NEG)
            m_new = jnp.maximum(m, jnp.max(s, -1, keepdims=True))
            p = jnp.exp(s - m_new)
            acc = acc * jnp.exp(m - m_new) + jnp.dot(p.astype(BF16), vj, preferred_element_type=F32)
            return m_new, acc

        m0 = jnp.full((tq, 1), NEG, F32)
        acc0 = jnp.zeros((tq, LANE), F32)
        _, acc = lax.fori_loop(0, i + 1, body, (m0, acc0))
        out_h = acc[:, :HEAD_DIM] / acc[:, HEAD_DIM:HEAD_DIM + 1]
        o_ref[0, :, h * HEAD_DIM:(h + 1) * HEAD_DIM] = out_h.astype(BF16)


def _mla(pa, pos, invf, cqg, ckvg, wuq, wk, wv, qg, kg, bsz, seq):
    hw = N_HEADS * LANE
    return pl.pallas_call(
        functools.partial(_mla_kernel, seq=seq),
        grid=(bsz, seq // MLA_TQ),
        in_specs=[pl.BlockSpec((1, seq, PA_W), lambda b, i: (b, 0, 0)),
                  pl.BlockSpec((1, seq, 1), lambda b, i: (b, 0, 0)),
                  _full((1, LANE)), _full((1, Q_RANK)), _full((1, KV_RANK)),
                  _full((Q_RANK, hw)), _full((KV_RANK, hw)), _full((KV_RANK, hw)),
                  _full((1, LANE)), _full((1, LANE))],
        out_specs=pl.BlockSpec((1, MLA_TQ, GROUP_W), lambda b, i: (b, i, 0)),
        out_shape=jax.ShapeDtypeStruct((bsz, seq, GROUP_W), BF16),
        scratch_shapes=[pltpu.VMEM((seq, hw), BF16)] * 3,
        compiler_params=_cparams(("parallel", "arbitrary")),
        name="mla",
    )(pa, pos, invf, cqg, ckvg, wuq, wk, wv, qg, kg)


ML_L = 128
ML_HALO = 8


def _mlstm_kernel(pb_ref, pg_ref, cw_ref, cb_ref, wq_ref, wk_ref, gb_ref, hng_ref, tri_ref,
                  o_ref, xpad, st, mst, fst):
    c = pl.program_id(1)
    L = ML_L

    @pl.when(c == 0)
    def _init():
        xpad[0:ML_HALO, :] = jnp.zeros((ML_HALO, GROUP_W), F32)
        st[...] = jnp.zeros(st.shape, F32)
        mst[...] = jnp.full(mst.shape, NEG, F32)
        fst[...] = jnp.zeros(fst.shape, F32)

    xqk = pb_ref[0, :, 0:GROUP_W].astype(F32)
    xpad[ML_HALO:ML_HALO + L, :] = xqk
    acc = cb_ref[...] + jnp.zeros((L, GROUP_W), F32)
    for j in range(CONV_B):
        off = ML_HALO - (CONV_B - 1) + j
        acc = acc + cw_ref[j:j + 1, :] * xpad[off:off + L, :]
    xpad[0:ML_HALO, :] = xqk[L - ML_HALO:, :]
    xc = (acc * _sigmoid(acc)).astype(BF16)
    q = jnp.dot(xc, wq_ref[...], preferred_element_type=F32)
    k = jnp.dot(xc, wk_ref[...], preferred_element_type=F32)

    gates = pg_ref[0] + gb_ref[...]
    ig = gates[:, :LANE]
    fg = gates[:, LANE:]
    lf = jnp.minimum(fg, 0.0) - jnp.log(1.0 + jnp.exp(-jnp.abs(fg)))
    fcol = _dot_exact_lhs(tri_ref[...], lf) + fst[...]
    fst[...] = fcol[L - 1:L, :]
    a_col = ig - fcol
    a_t = a_col.T

    rowi = lax.broadcasted_iota(jnp.int32, (L, L), 0)
    coli = lax.broadcasted_iota(jnp.int32, (L, L), 1)
    causal = coli <= rowi
    lane_v = lax.broadcasted_iota(jnp.int32, (L, LANE), 1)
    lane_w = lax.broadcasted_iota(jnp.int32, (L, GROUP_W), 1)

    mts, mns, mcs = [], [], []
    for h in range(N_HEADS):
        amat = jnp.where(causal, a_t[h:h + 1, :], NEG)
        mc = mst[:, h:h + 1]
        mt = jnp.maximum(jnp.max(amat, -1, keepdims=True), mc)
        mts.append((amat, mt))
        mcs.append(mc)
        mns.append(mt[L - 1:L, :])

    wg = jnp.exp(a_col[:, N_HEADS - 1:N_HEADS] - mns[N_HEADS - 1])
    for h in range(N_HEADS - 2, -1, -1):
        wg = jnp.where(lane_w < (h + 1) * HEAD_DIM, jnp.exp(a_col[:, h:h + 1] - mns[h]), wg)
    kw_t = (k * wg).T.astype(BF16)

    for h in range(N_HEADS):
        hs = slice(h * HEAD_DIM, (h + 1) * HEAD_DIM)
        amat, mt = mts[h]
        mc, mn = mcs[h], mns[h]
        p = jnp.exp(amat - mt)
        qh = q[:, hs].astype(BF16)
        kh = k[:, hs].astype(BF16)
        qk = lax.dot_general(qh, kh, NT, preferred_element_type=F32)
        w = (p * qk).astype(BF16)
        vext = jnp.where(lane_v == HEAD_DIM, 1.0,
                         pb_ref[0, :, GROUP_W + h * LANE:GROUP_W + (h + 1) * LANE].astype(F32)).astype(BF16)
        sth = st[h]
        nd = (jnp.dot(w, vext, preferred_element_type=F32)
              + jnp.exp(mc - mt) * jnp.dot(qh, sth.astype(BF16), preferred_element_type=F32))
        den = nd[:, HEAD_DIM:HEAD_DIM + 1]
        floor = jnp.exp(-(fcol[:, h:h + 1] + mt))
        hh = nd[:, :HEAD_DIM] / jnp.maximum(jnp.abs(den), floor)
        hh = hh * lax.rsqrt(jnp.mean(hh * hh, -1, keepdims=True) + EPS) * hng_ref[:, hs]
        xo = pb_ref[0, :, GROUP_W + N_HEADS * LANE + h * HEAD_DIM:
                    GROUP_W + N_HEADS * LANE + (h + 1) * HEAD_DIM].astype(F32)
        o_ref[0, :, hs] = (_sigmoid(xo) * hh).astype(BF16)
        st[h] = jnp.exp(mc - mn) * sth + jnp.dot(kw_t[hs, :], vext, preferred_element_type=F32)
        mst[:, h:h + 1] = mn


def _mlstm(pb, pg, cw, cb, wq, wk, gb, hng, tri, bsz, seq):
    L = ML_L
    return pl.pallas_call(
        _mlstm_kernel,
        grid=(bsz, seq // L),
        in_specs=[pl.BlockSpec((1, L, PB_W), lambda b, c: (b, c, 0)),
                  pl.BlockSpec((1, L, PG_W), lambda b, c: (b, c, 0)),
                  _full((CONV_B, GROUP_W)), _full((1, GROUP_W)),
                  _full((GROUP_W, GROUP_W)), _full((GROUP_W, GROUP_W)),
                  _full((1, PG_W)), _full((1, GROUP_W)), _full((L, L))],
        out_specs=pl.BlockSpec((1, L, GROUP_W), lambda b, c: (b, c, 0)),
        out_shape=jax.ShapeDtypeStruct((bsz, seq, GROUP_W), BF16),
        scratch_shapes=[pltpu.VMEM((L + ML_HALO, GROUP_W), F32),
                        pltpu.VMEM((N_HEADS, HEAD_DIM, LANE), F32),
                        pltpu.VMEM((1, LANE), F32),
                        pltpu.VMEM((1, LANE), F32)],
        compiler_params=_cparams(("parallel", "arbitrary")),
        name="mlstm",
    )(pb, pg, cw, cb, wq, wk, gb, hng, tri)


SP_STEP = 4


def _spatial_kernel(pc_ref, g_ref, b_ref, ws_ref, bias_ref, o_ref):
    T = SP_CHUNK
    rowi = lax.broadcasted_iota(jnp.int32, (T, N_HEADS * T), 0)
    coli = lax.broadcasted_iota(jnp.int32, (T, N_HEADS * T), 1)
    ws = jnp.where((coli & (T - 1)) <= rowi, ws_ref[...], 0.0).astype(BF16)
    rgrp = lax.broadcasted_iota(jnp.int32, (T, GROUP_W), 1) // HEAD_DIM
    for cidx in range(SP_STEP):
        rs = slice(cidx * T, (cidx + 1) * T)
        u = _gelu(pc_ref[rs, 0:GROUP_W].astype(F32))
        gv = _gelu(pc_ref[rs, GROUP_W:].astype(F32))
        mu = jnp.mean(gv, -1, keepdims=True)
        dv = gv - mu
        var = jnp.mean(dv * dv, -1, keepdims=True)
        vn = (dv * lax.rsqrt(var + EPS) * g_ref[...] + b_ref[...]).astype(BF16)
        zero = jnp.zeros_like(vn)
        vbig = jnp.concatenate([jnp.where(rgrp == g, vn, zero) for g in range(N_HEADS)], axis=0)
        sg = jnp.dot(ws, vbig, preferred_element_type=F32) + bias_ref[...]
        o_ref[rs, :] = (u * sg).astype(BF16)


def _spatial(pc, g, b, ws_cat, bias_full):
    t = pc.shape[0]
    tm = SP_STEP * SP_CHUNK
    return pl.pallas_call(
        _spatial_kernel,
        grid=(t // tm,),
        in_specs=[pl.BlockSpec((tm, PC_W), lambda i: (i, 0)),
                  _full((1, GROUP_W)), _full((1, GROUP_W)),
                  _full((SP_CHUNK, N_HEADS * SP_CHUNK)), _full((SP_CHUNK, GROUP_W))],
        out_specs=pl.BlockSpec((tm, GROUP_W), lambda i: (i, 0)),
        out_shape=jax.ShapeDtypeStruct((t, GROUP_W), BF16),
        compiler_params=_cparams(("parallel",)),
        name="spatial",
    )(pc, g, b, ws_cat, bias_full)


CV_T = 256
CV_HALO = 32


def _convmod_kernel(pd_ref, w_ref, b_ref, avg_ref, g_ref, be_ref, o_ref, ypad):
    i = pl.program_id(1)
    T = CV_T

    @pl.when(i == 0)
    def _init():
        ypad[0:CV_HALO, :] = jnp.zeros((CV_HALO, GROUP_W), F32)

    a = pd_ref[0, :, 0:GROUP_W].astype(F32)
    b = pd_ref[0, :, GROUP_W:].astype(F32)
    y = a * _sigmoid(b)
    ypad[CV_HALO:CV_HALO + T, :] = y
    acc = b_ref[...] + jnp.zeros((T, GROUP_W), F32)
    for j in range(CONV_D):
        off = CV_HALO - (CONV_D - 1) + j
        acc = acc + w_ref[j:j + 1, :] * ypad[off:off + T, :]
    ypad[0:CV_HALO, :] = y[T - CV_HALO:, :]
    mu = _dot_exact_rhs(acc, avg_ref[...])
    dv = acc - mu
    var = _dot_exact_rhs(dv * dv, avg_ref[...])
    yn = dv * lax.rsqrt(var + EPS) * g_ref[...] + be_ref[...]
    o_ref[0] = (yn * _sigmoid(yn)).astype(BF16)


def _convmod(pd, w, b, avg, g, be, bsz, seq):
    return pl.pallas_call(
        _convmod_kernel,
        grid=(bsz, seq // CV_T),
        in_specs=[pl.BlockSpec((1, CV_T, PD_W), lambda bb, i: (bb, i, 0)),
                  _full((CONV_D, GROUP_W)), _full((1, GROUP_W)), _full((GROUP_W, GROUP_W)),
                  _full((1, GROUP_W)), _full((1, GROUP_W))],
        out_specs=pl.BlockSpec((1, CV_T, GROUP_W), lambda bb, i: (bb, i, 0)),
        out_shape=jax.ShapeDtypeStruct((bsz, seq, GROUP_W), BF16),
        scratch_shapes=[pltpu.VMEM((CV_T + CV_HALO, GROUP_W), F32)],
        compiler_params=_cparams(("parallel", "arbitrary")),
        name="convmod",
    )(pd, w, b, avg, g, be)


OP_TM = 512
PQ_W = P_HEADS * P_HALF
KROWS = P_KEYS * P_HEADS


def _outproj_kernel(x_ref, ma_ref, mb_ref, mc_ref, md_ref, wo_ref, g_ref, wpq_ref, k1_ref, k2_ref,
                    k2h_ref, h_ref, xnt_ref, s1_ref, s2_ref, s2h_ref):
    acc = x_ref[...]
    for g, m in enumerate((ma_ref, mb_ref, mc_ref, md_ref)):
        acc = acc + jnp.dot(m[...], wo_ref[g * GROUP_W:(g + 1) * GROUP_W, :], preferred_element_type=F32)
    h_ref[...] = acc
    hn = acc * lax.rsqrt(jnp.mean(acc * acc, -1, keepdims=True) + EPS) * g_ref[...]
    xnt_ref[...] = hn.T.astype(BF16)
    q = jnp.dot(hn.astype(BF16), wpq_ref[...], preferred_element_type=F32).astype(BF16)
    s1_ref[...] = lax.dot_general(k1_ref[...], q[:, :PQ_W], NT, preferred_element_type=F32)
    s2_ref[...] = lax.dot_general(k2_ref[...], q[:, PQ_W:], NT, preferred_element_type=F32)
    for h in range(P_HEADS):
        qh = q[:, PQ_W + h * P_HALF:PQ_W + (h + 1) * P_HALF]
        s2h_ref[h * P_KEYS:(h + 1) * P_KEYS, :] = lax.dot_general(
            k2h_ref[h], qh, NT, preferred_element_type=F32)


def _outproj(x, ma, mb, mc, md, wo, g, wpq, k1, k2, k2h):
    t = x.shape[0]
    tm = OP_TM
    mix = pl.BlockSpec((tm, GROUP_W), lambda i: (i, 0))
    tcol = pl.BlockSpec((KROWS, tm), lambda i: (0, i))
    return pl.pallas_call(
        _outproj_kernel,
        grid=(t // tm,),
        in_specs=[pl.BlockSpec((tm, D_MODEL), lambda i: (i, 0)), mix, mix, mix, mix,
                  _full((D_MODEL, D_MODEL)), _full((1, D_MODEL)), _full((D_MODEL, 2 * PQ_W)),
                  _full((KROWS, PQ_W)), _full((KROWS, PQ_W)), _full((P_HEADS, P_KEYS, P_HALF))],
        out_specs=[pl.BlockSpec((tm, D_MODEL), lambda i: (i, 0)),
                   pl.BlockSpec((D_MODEL, tm), lambda i: (0, i)), tcol, tcol, tcol],
        out_shape=[jax.ShapeDtypeStruct((t, D_MODEL), F32),
                   jax.ShapeDtypeStruct((D_MODEL, t), BF16),
                   jax.ShapeDtypeStruct((KROWS, t), F32),
                   jax.ShapeDtypeStruct((KROWS, t), F32),
                   jax.ShapeDtypeStruct((KROWS, t), F32)],
        compiler_params=_cparams(("parallel",)),
        name="outproj",
    )(x, ma, mb, mc, md, wo, g, wpq, k1, k2, k2h)


RT_T = 256
N_SORT = P_TOPK + 1


def _sorted_top(ref, c0):
    tops = []
    prev = jnp.full((P_HEADS, LANE), jnp.inf, F32)
    lanes = pl.ds(c0, LANE)
    for _ in range(N_SORT):
        def body(n, accs, prev=prev):
            new = []
            for u in range(4):
                r0 = pl.multiple_of((n * 4 + u) * P_HEADS, P_HEADS)
                v = ref[pl.ds(r0, P_HEADS), lanes]
                new.append(jnp.maximum(accs[u], jnp.where(v < prev, v, -jnp.inf)))
            return tuple(new)

        init = tuple(jnp.full((P_HEADS, LANE), -jnp.inf, F32) for _ in range(4))
        a = lax.fori_loop(0, P_KEYS // 4, body, init, unroll=4)
        cur = jnp.maximum(jnp.maximum(a[0], a[1]), jnp.maximum(a[2], a[3]))
        tops.append(cur)
        prev = cur
    return tops


def _route_kernel(s1_ref, s2_ref, s2h_ref, thr_ref, w_ref, p2h_ref):
    for g in range(RT_T // LANE):
        c0 = g * LANE
        lanes = pl.ds(c0, LANE)
        t1 = _sorted_top(s1_ref, c0)
        t2 = _sorted_top(s2_ref, c0)
        cands = [t1[a] + t2[b] for a in range(N_SORT) for b in range(N_SORT)
                 if (a + 1) * (b + 1) <= N_SORT]
        prev = jnp.full((P_HEADS, LANE), jnp.inf, F32)
        last = prev
        for _ in range(N_SORT):
            cur = jnp.full((P_HEADS, LANE), -jnp.inf, F32)
            for cnd in cands:
                cur = jnp.maximum(cur, jnp.where(cnd < prev, cnd, -jnp.inf))
            last, prev = prev, cur
        tau = 0.5 * (last + prev)
        cmax = t1[0] + t2[0]
        z = jnp.zeros((P_HEADS, LANE), F32)
        for cnd in cands:
            z = z + jnp.where(cnd >= tau, jnp.exp(cnd - cmax), 0.0)
        zinv = 1.0 / z

        def body(n, carry):
            r0 = pl.multiple_of(n * P_HEADS, P_HEADS)
            s1 = s1_ref[pl.ds(r0, P_HEADS), lanes]
            thr_ref[pl.ds(r0, P_HEADS), lanes] = tau - s1
            w_ref[pl.ds(r0, P_HEADS), lanes] = jnp.exp(s1 - t1[0]) * zinv
            return carry

        lax.fori_loop(0, P_KEYS, body, 0, unroll=8)
        for h in range(P_HEADS):
            rows = slice(h * P_KEYS, (h + 1) * P_KEYS)
            p2h_ref[rows, lanes] = jnp.exp(s2h_ref[rows, lanes] - t2[0][h:h + 1, :])


def _route(s1, s2, s2h):
    t = s1.shape[1]
    blk = pl.BlockSpec((KROWS, RT_T), lambda i: (0, i))
    return pl.pallas_call(
        _route_kernel,
        grid=(t // RT_T,),
        in_specs=[blk, blk, blk],
        out_specs=[blk, blk, blk],
        out_shape=[jax.ShapeDtypeStruct((KROWS, t), F32)] * 3,
        compiler_params=_cparams(("parallel",)),
        name="route",
    )(s1, s2, s2h)


PE_TT = 512
PE_EB = 512
PE_K1 = PE_EB // P_KEYS


def _peer_kernel(xnt_ref, u_ref, vt_ref, thr_ref, w_ref, s2h_ref, p2h_ref, h_ref, o_ref, acc, ht):
    j = pl.program_id(1)

    @pl.when(j == 0)
    def _init():
        acc[...] = jnp.zeros(acc.shape, F32)

    a = jnp.dot(u_ref[...], xnt_ref[...], preferred_element_type=F32)
    for c in range(PE_K1):
        e1 = j * PE_K1 + c
        gate = jnp.zeros((P_KEYS, PE_TT), F32)
        for h in range(P_HEADS):
            rows = slice(h * P_KEYS, (h + 1) * P_KEYS)
            thr = thr_ref[e1, h:h + 1, :]
            wgt = w_ref[e1, h:h + 1, :]
            gate = gate + jnp.where(s2h_ref[rows, :] >= thr, p2h_ref[rows, :], 0.0) * wgt
        ac = a[c * P_KEYS:(c + 1) * P_KEYS, :]
        ht[c * P_KEYS:(c + 1) * P_KEYS, :] = (_gelu(ac) * gate).astype(BF16)
    acc[...] += jnp.dot(vt_ref[...], ht[...], preferred_element_type=F32)

    @pl.when(j == pl.num_programs(1) - 1)
    def _fin():
        o_ref[...] = h_ref[...] + acc[...].T


def _peer(xnt, u, vt, thr, w, s2h, p2h, hres):
    t = xnt.shape[1]
    tok = lambda r: pl.BlockSpec((r, PE_TT), lambda i, j: (0, i))
    tok3 = pl.BlockSpec((P_KEYS, P_HEADS, PE_TT), lambda i, j: (0, 0, i))
    return pl.pallas_call(
        _peer_kernel,
        grid=(t // PE_TT, N_EXPERTS // PE_EB),
        in_specs=[tok(D_MODEL),
                  pl.BlockSpec((PE_EB, D_MODEL), lambda i, j: (j, 0)),
                  pl.BlockSpec((D_MODEL, PE_EB), lambda i, j: (0, j)),
                  tok3, tok3, tok(KROWS), tok(KROWS),
                  pl.BlockSpec((PE_TT, D_MODEL), lambda i, j: (i, 0))],
        out_specs=pl.BlockSpec((PE_TT, D_MODEL), lambda i, j: (i, 0)),
        out_shape=jax.ShapeDtypeStruct((t, D_MODEL), F32),
        scratch_shapes=[pltpu.VMEM((D_MODEL, PE_TT), F32), pltpu.VMEM((PE_EB, PE_TT), BF16)],
        compiler_params=_cparams(("parallel", "arbitrary")),
        name="peer",
    )(xnt, u, vt, thr, w, s2h, p2h, hres)


def _pad_cols(w, width):
    return jnp.pad(w, ((0, 0), (0, width - w.shape[1])))


def _prep_layer(l, p):
    w_in = p["w_in"][l]
    o = 0
    cuts = {}
    for name, wdt in (("a", Q_RANK + KV_RANK + ROPE), ("bqk", GROUP_W), ("bv", GROUP_W), ("bo", GROUP_W),
                      ("bi", N_HEADS), ("bf", N_HEADS), ("c", 2 * GROUP_W), ("d", 2 * GROUP_W)):
        cuts[name] = w_in[:, o:o + wdt]
        o += wdt
    bv = cuts["bv"].reshape(D_MODEL, N_HEADS, HEAD_DIM)
    bv = jnp.pad(bv, ((0, 0), (0, 0), (0, LANE - HEAD_DIM))).reshape(D_MODEL, N_HEADS * LANE)
    w_all = jnp.concatenate([
        _pad_cols(cuts["a"], PA_W), cuts["bqk"], bv, cuts["bo"],
        _pad_cols(cuts["bi"], LANE), _pad_cols(cuts["bf"], LANE), cuts["c"], cuts["d"]], axis=1).astype(BF16)

    def head_pad(w, width):
        return jnp.pad(w, ((0, 0), (0, 0), (0, LANE - width))).reshape(w.shape[0], N_HEADS * LANE)

    wuq = head_pad(p["a_w_uq"][l].reshape(Q_RANK, N_HEADS, QK_DIM), QK_DIM).astype(BF16)
    wukv = p["a_w_ukv"][l].reshape(KV_RANK, N_HEADS, NOPE + HEAD_DIM)
    wk = head_pad(wukv[:, :, :NOPE], NOPE).astype(BF16)
    wv = head_pad(wukv[:, :, NOPE:], HEAD_DIM).astype(BF16)
    qg = _pad_cols(p["a_qn_g"][l][None, :] * (QK_DIM ** -0.5), LANE)
    kg = _pad_cols(p["a_kn_g"][l][None, :], LANE)

    eye = jnp.eye(N_HEADS, dtype=F32)
    wq_bd = jnp.einsum("hde,hg->hdge", p["b_w_q"][l], eye).reshape(GROUP_W, GROUP_W).astype(BF16)
    wk_bd = (jnp.einsum("hde,hg->hdge", p["b_w_k"][l], eye).reshape(GROUP_W, GROUP_W)
             * (HEAD_DIM ** -0.5)).astype(BF16)
    gb = jnp.concatenate([_pad_cols(p["b_b_i"][l][None, :], LANE), _pad_cols(p["b_b_f"][l][None, :], LANE)], axis=1)

    ws_cat = p["c_w_s"][l].transpose(1, 0, 2).reshape(SP_CHUNK, N_HEADS * SP_CHUNK)
    bias_full = jnp.repeat(p["c_b_s"][l].T, HEAD_DIM, axis=1)

    wpq = p["p_w_q"][l].reshape(D_MODEL, P_HEADS, 2, P_HALF).transpose(0, 2, 1, 3).reshape(D_MODEL, 2 * PQ_W)
    keys = p["p_sub_keys"][l]
    eye8 = jnp.eye(P_HEADS, dtype=F32)
    kmat = [jnp.einsum("hnd,hg->nhgd", keys[:, s], eye8).reshape(KROWS, PQ_W).astype(BF16) for s in range(2)]
    return dict(
        norm1_g=p["norm1_g"][l][None, :], w_all=w_all,
        cqg=p["a_cq_g"][l][None, :], ckvg=p["a_ckv_g"][l][None, :], wuq=wuq, wk=wk, wv=wv, qg=qg, kg=kg,
        b_cw=p["b_conv_w"][l], b_cb=p["b_conv_b"][l][None, :], wq_bd=wq_bd, wk_bd=wk_bd, gb=gb,
        hng=p["b_hn_g"][l].reshape(1, GROUP_W),
        c_g=p["c_ln_g"][l][None, :], c_b=p["c_ln_b"][l][None, :], ws_cat=ws_cat, bias_full=bias_full,
        d_w=p["d_dw_w"][l], d_b=p["d_dw_b"][l][None, :],
        d_g=p["d_cn_g"][l].reshape(1, GROUP_W), d_be=p["d_cn_b"][l].reshape(1, GROUP_W),
        w_out=p["w_out"][l].astype(BF16), norm2_g=p["norm2_g"][l][None, :], wpq=wpq.astype(BF16),
        k1=kmat[0], k2=kmat[1], k2h=keys[:, 1].astype(BF16),
        u=p["p_u"][l].astype(BF16), vt=p["p_v"][l].T.astype(BF16),
    )


def _layer(x, pos3, consts, w, bsz, seq):
    t = bsz * seq
    pa, pb, pg, pc, pd = _inproj(x, w["norm1_g"], w["w_all"])
    r3 = lambda a: a.reshape(bsz, seq, a.shape[-1])
    mix_a = _mla(r3(pa), pos3, consts["invf"], w["cqg"], w["ckvg"], w["wuq"], w["wk"], w["wv"],
                 w["qg"], w["kg"], bsz, seq).reshape(t, GROUP_W)
    mix_b = _mlstm(r3(pb), r3(pg), w["b_cw"], w["b_cb"], w["wq_bd"], w["wk_bd"], w["gb"], w["hng"],
                   consts["tri"], bsz, seq).reshape(t, GROUP_W)
    mix_c = _spatial(pc, w["c_g"], w["c_b"], w["ws_cat"], w["bias_full"])
    mix_d = _convmod(r3(pd), w["d_w"], w["d_b"], consts["avg"], w["d_g"], w["d_be"], bsz, seq).reshape(t, GROUP_W)
    h, xnt, s1, s2, s2h = _outproj(x, mix_a, mix_b, mix_c, mix_d, w["w_out"], w["norm2_g"], w["wpq"],
                                   w["k1"], w["k2"], w["k2h"])
    thr, wgt, p2h = _route(s1, s2, s2h)
    shp = (P_KEYS, P_HEADS, t)
    return _peer(xnt, w["u"], w["vt"], thr.reshape(shp), wgt.reshape(shp), s2h, p2h, h)


def _consts():
    half = ROPE // 2
    inv_freq = ROPE_BASE ** (-jnp.arange(0, ROPE, 2, dtype=F32) / ROPE)
    invf = jnp.zeros((1, LANE), F32)
    invf = invf.at[0, NOPE:NOPE + half].set(inv_freq).at[0, NOPE + half:QK_DIM].set(inv_freq)
    tri = jnp.tril(jnp.ones((ML_L, ML_L), F32)).astype(BF16)
    grp = jnp.arange(GROUP_W) // HEAD_DIM
    avg = ((grp[:, None] == grp[None, :]).astype(F32) / HEAD_DIM).astype(BF16)
    return dict(invf=invf, tri=tri, avg=avg)


def kernel(x, positions, norm1_g, w_in, a_cq_g, a_ckv_g, a_w_uq, a_w_ukv, a_qn_g, a_kn_g, b_conv_w, b_conv_b, b_w_q, b_w_k, b_b_i, b_b_f, b_hn_g, c_ln_g, c_ln_b, c_w_s, c_b_s, d_dw_w, d_dw_b, d_cn_g, d_cn_b, w_out, norm2_g, p_w_q, p_sub_keys, p_u, p_v):
    params = dict(norm1_g=norm1_g, w_in=w_in, a_cq_g=a_cq_g, a_ckv_g=a_ckv_g, a_w_uq=a_w_uq, a_w_ukv=a_w_ukv,
                  a_qn_g=a_qn_g, a_kn_g=a_kn_g, b_conv_w=b_conv_w, b_conv_b=b_conv_b, b_w_q=b_w_q, b_w_k=b_w_k,
                  b_b_i=b_b_i, b_b_f=b_b_f, b_hn_g=b_hn_g, c_ln_g=c_ln_g, c_ln_b=c_ln_b, c_w_s=c_w_s,
                  c_b_s=c_b_s, d_dw_w=d_dw_w, d_dw_b=d_dw_b, d_cn_g=d_cn_g, d_cn_b=d_cn_b, w_out=w_out,
                  norm2_g=norm2_g, p_w_q=p_w_q, p_sub_keys=p_sub_keys, p_u=p_u, p_v=p_v)
    bsz, seq, d = x.shape
    consts = _consts()
    pos3 = positions.reshape(bsz, seq, 1)
    xf = x.reshape(bsz * seq, d)
    for l in range(norm1_g.shape[0]):
        xf = _layer(xf, pos3, consts, _prep_layer(l, params), bsz, seq)
    return xf.reshape(bsz, seq, d)
```

```python
import functools

import jax
import jax.numpy as jnp
from jax import lax
from jax.experimental import pallas as pl
from jax.experimental.pallas import tpu as pltpu

F32 = jnp.float32
BF16 = jnp.bfloat16

D_MODEL = 1024
N_HEADS = 4
HEAD_DIM = 64
GROUP_W = N_HEADS * HEAD_DIM
Q_RANK, KV_RANK = 256, 128
NOPE, ROPE = 64, 32
QK_DIM = NOPE + ROPE
ROPE_BASE = 10000.0
LANE = 128
CONV_B = 4
CONV_D = 31
SP_CHUNK = 128
P_HEADS = 8
P_KEYS = 128
P_TOPK = 16
P_HALF = 64
N_EXPERTS = P_KEYS * P_KEYS
EPS = 1e-6
NEG = -1e30
VMEM_LIMIT = 56 * 1024 * 1024

NT = (((1,), (1,)), ((), ()))


def _cparams(sem):
    return pltpu.CompilerParams(dimension_semantics=sem, vmem_limit_bytes=VMEM_LIMIT)


def _full(shape):
    n = len(shape)
    return pl.BlockSpec(shape, lambda *_: (0,) * n)


def _split3(a):
    hi = a.astype(BF16)
    r1 = a - hi.astype(F32)
    mid = r1.astype(BF16)
    lo = (r1 - mid.astype(F32)).astype(BF16)
    return hi, mid, lo


def _dot_exact_rhs(a, m_bf16):
    hi, mid, lo = _split3(a)
    d = lambda t: jnp.dot(t, m_bf16, preferred_element_type=F32)
    return d(hi) + d(mid) + d(lo)


def _dot_exact_lhs(m_bf16, a):
    hi, mid, lo = _split3(a)
    d = lambda t: jnp.dot(m_bf16, t, preferred_element_type=F32)
    return d(hi) + d(mid) + d(lo)


def _sigmoid(x):
    return 1.0 / (1.0 + jnp.exp(-x))


def _gelu(x):
    return 0.5 * x * (1.0 + jnp.tanh(0.7978845608028654 * (x + 0.044715 * (x * x * x))))


PA_W, PB_W, PG_W, PC_W, PD_W = 512, 1024, 256, 512, 512
IN_COLS = (0, PA_W, PA_W + PB_W, PA_W + PB_W + PG_W, PA_W + PB_W + PG_W + PC_W,
           PA_W + PB_W + PG_W + PC_W + PD_W)


def _inproj_kernel(x_ref, g_ref, w_ref, pa_ref, pb_ref, pg_ref, pc_ref, pd_ref):
    x = x_ref[...]
    hn = (x * lax.rsqrt(jnp.mean(x * x, axis=-1, keepdims=True) + EPS) * g_ref[...]).astype(BF16)
    outs = (pa_ref, pb_ref, pg_ref, pc_ref, pd_ref)
    for k, o in enumerate(outs):
        y = jnp.dot(hn, w_ref[:, IN_COLS[k]:IN_COLS[k + 1]], preferred_element_type=F32)
        o[...] = y.astype(o.dtype)


def _inproj(x, g, w_all, tm=512):
    t = x.shape[0]
    widths = (PA_W, PB_W, PG_W, PC_W, PD_W)
    dts = (BF16, BF16, F32, BF16, BF16)
    return pl.pallas_call(
        _inproj_kernel,
        grid=(t // tm,),
        in_specs=[pl.BlockSpec((tm, D_MODEL), lambda i: (i, 0)),
                  _full((1, D_MODEL)), _full((D_MODEL, IN_COLS[-1]))],
        out_specs=[pl.BlockSpec((tm, w), lambda i: (i, 0)) for w in widths],
        out_shape=[jax.ShapeDtypeStruct((t, w), d) for w, d in zip(widths, dts)],
        compiler_params=_cparams(("parallel",)),
        name="inproj",
    )(x, g, w_all)


MLA_TQ = 256
MLA_PRO = 512


def _mla_kernel(pa_ref, pos_ref, invf_ref, cqg_ref, ckvg_ref, wuq_ref, wk_ref, wv_ref,
                qg_ref, kg_ref, o_ref, q_s, k_s, v_s, *, seq):
    i = pl.program_id(1)
    tq = MLA_TQ

    @pl.when(i == 0)
    def _prologue():
        def chunk(c, carry):
            r0 = pl.multiple_of(c * MLA_PRO, MLA_PRO)
            pa = pa_ref[0, pl.ds(r0, MLA_PRO), :]
            cq = pa[:, 0:Q_RANK].astype(F32)
            ckv = pa[:, Q_RANK:Q_RANK + KV_RANK].astype(F32)
            krp = pa[:, Q_RANK + KV_RANK:].astype(F32)
            cqn = (cq * lax.rsqrt(jnp.mean(cq * cq, -1, keepdims=True) + EPS) * cqg_ref[...]).astype(BF16)
            ckvn = (ckv * lax.rsqrt(jnp.mean(ckv * ckv, -1, keepdims=True) + EPS) * ckvg_ref[...]).astype(BF16)
            q = jnp.dot(cqn, wuq_ref[...], preferred_element_type=F32)
            kn = jnp.dot(ckvn, wk_ref[...], preferred_element_type=F32)
            v = jnp.dot(ckvn, wv_ref[...], preferred_element_type=F32)
            pos = pos_ref[0, pl.ds(r0, MLA_PRO), :].astype(F32)
            ang = pos * invf_ref[...]
            lane = lax.broadcasted_iota(jnp.int32, (MLA_PRO, LANE), 1)
            cosv = jnp.cos(ang)
            sinv = jnp.sin(ang)
            half = ROPE // 2
            sin_hi = jnp.where((lane >= NOPE + half) & (lane < QK_DIM), sinv, 0.0)
            sin_lo = jnp.where((lane >= NOPE) & (lane < NOPE + half), -sinv, 0.0)
            kr_sh = pltpu.roll(krp, NOPE, 1)

            def norm_rope(t, g):
                t = t * lax.rsqrt(jnp.sum(t * t, -1, keepdims=True) * (1.0 / QK_DIM) + EPS) * g
                return (t * cosv + pltpu.roll(t, half, 1) * sin_hi
                        + pltpu.roll(t, LANE - half, 1) * sin_lo)

            for h in range(N_HEADS):
                sl = slice(h * LANE, (h + 1) * LANE)
                q_s[pl.ds(r0, MLA_PRO), sl] = norm_rope(q[:, sl], qg_ref[...]).astype(BF16)
                k_s[pl.ds(r0, MLA_PRO), sl] = norm_rope(kn[:, sl] + kr_sh, kg_ref[...]).astype(BF16)
                v_s[pl.ds(r0, MLA_PRO), sl] = jnp.where(lane == HEAD_DIM, 1.0, v[:, sl]).astype(BF16)
            return carry

        lax.fori_loop(0, seq // MLA_PRO, chunk, 0)

    q0 = pl.multiple_of(i * tq, tq)
    row = lax.broadcasted_iota(jnp.int32, (tq, tq), 0) + i * tq
    col = lax.broadcasted_iota(jnp.int32, (tq, tq), 1)
    for h in range(N_HEADS):
        sl = slice(h * LANE, (h + 1) * LANE)
        qh = q_s[pl.ds(q0, tq), sl]

        def body(j, carry, sl=sl, qh=qh):
            m, acc = carry
            k0 = pl.multiple_of(j * tq, tq)
            kj = k_s[pl.ds(k0, tq), sl]
            vj = v_s[pl.ds(k0, tq), sl]
            s = lax.dot_general(qh, kj, NT, preferred_element_type=F32)
            s = jnp.where(col + j * tq <= row, s, NEG)
            m_new = jnp.maximum(m, jnp.max(s, -1, keepdims=True))
            p = jnp.exp(s - m_new)
            acc = acc * jnp.exp(m - m_new) + jnp.dot(p.astype(BF16), vj, preferred_element_type=F32)
            return m_new, acc

        m0 = jnp.full((tq, 1), NEG, F32)
        acc0 = jnp.zeros((tq, LANE), F32)
        _, acc = lax.fori_loop(0, i + 1, body, (m0, acc0))
        out_h = acc[:, :HEAD_DIM] / acc[:, HEAD_DIM:HEAD_DIM + 1]
        o_ref[0, :, h * HEAD_DIM:(h + 1) * HEAD_DIM] = out_h.astype(BF16)


def _mla(pa, pos, invf, cqg, ckvg, wuq, wk, wv, qg, kg, bsz, seq):
    hw = N_HEADS * LANE
    return pl.pallas_call(
        functools.partial(_mla_kernel, seq=seq),
        grid=(bsz, seq // MLA_TQ),
        in_specs=[pl.BlockSpec((1, seq, PA_W), lambda b, i: (b, 0, 0)),
                  pl.BlockSpec((1, seq, 1), lambda b, i: (b, 0, 0)),
                  _full((1, LANE)), _full((1, Q_RANK)), _full((1, KV_RANK)),
                  _full((Q_RANK, hw)), _full((KV_RANK, hw)), _full((KV_RANK, hw)),
                  _full((1, LANE)), _full((1, LANE))],
        out_specs=pl.BlockSpec((1, MLA_TQ, GROUP_W), lambda b, i: (b, i, 0)),
        out_shape=jax.ShapeDtypeStruct((bsz, seq, GROUP_W), BF16),
        scratch_shapes=[pltpu.VMEM((seq, hw), BF16)] * 3,
        compiler_params=_cparams(("parallel", "arbitrary")),
        name="mla",
    )(pa, pos, invf, cqg, ckvg, wuq, wk, wv, qg, kg)


ML_L = 128
ML_HALO = 8


def _mlstm_kernel(pb_ref, pg_ref, cw_ref, cb_ref, wq_ref, wk_ref, gb_ref, hng_ref, tri_ref,
                  o_ref, xpad, st, mst, fst):
    c = pl.program_id(1)
    L = ML_L

    @pl.when(c == 0)
    def _init():
        xpad[0:ML_HALO, :] = jnp.zeros((ML_HALO, GROUP_W), F32)
        st[...] = jnp.zeros(st.shape, F32)
        mst[...] = jnp.full(mst.shape, NEG, F32)
        fst[...] = jnp.zeros(fst.shape, F32)

    xqk = pb_ref[0, :, 0:GROUP_W].astype(F32)
    xpad[ML_HALO:ML_HALO + L, :] = xqk
    acc = cb_ref[...] + jnp.zeros((L, GROUP_W), F32)
    for j in range(CONV_B):
        off = ML_HALO - (CONV_B - 1) + j
        acc = acc + cw_ref[j:j + 1, :] * xpad[off:off + L, :]
    xpad[0:ML_HALO, :] = xqk[L - ML_HALO:, :]
    xc = (acc * _sigmoid(acc)).astype(BF16)
    q = jnp.dot(xc, wq_ref[...], preferred_element_type=F32)
    k = jnp.dot(xc, wk_ref[...], preferred_element_type=F32)

    gates = pg_ref[0] + gb_ref[...]
    ig = gates[:, :LANE]
    fg = gates[:, LANE:]
    lf = jnp.minimum(fg, 0.0) - jnp.log(1.0 + jnp.exp(-jnp.abs(fg)))
    fcol = _dot_exact_lhs(tri_ref[...], lf) + fst[...]
    fst[...] = fcol[L - 1:L, :]
    a_col = ig - fcol
    a_t = a_col.T

    rowi = lax.broadcasted_iota(jnp.int32, (L, L), 0)
    coli = lax.broadcasted_iota(jnp.int32, (L, L), 1)
    causal = coli <= rowi
    lane_v = lax.broadcasted_iota(jnp.int32, (L, LANE), 1)
    lane_w = lax.broadcasted_iota(jnp.int32, (L, GROUP_W), 1)

    mts, mns, mcs = [], [], []
    for h in range(N_HEADS):
        amat = jnp.where(causal, a_t[h:h + 1, :], NEG)
        mc = mst[:, h:h + 1]
        mt = jnp.maximum(jnp.max(amat, -1, keepdims=True), mc)
        mts.append((amat, mt))
        mcs.append(mc)
        mns.append(mt[L - 1:L, :])

    wg = jnp.exp(a_col[:, N_HEADS - 1:N_HEADS] - mns[N_HEADS - 1])
    for h in range(N_HEADS - 2, -1, -1):
        wg = jnp.where(lane_w < (h + 1) * HEAD_DIM, jnp.exp(a_col[:, h:h + 1] - mns[h]), wg)
    kw_t = (k * wg).T.astype(BF16)

    for h in range(N_HEADS):
        hs = slice(h * HEAD_DIM, (h + 1) * HEAD_DIM)
        amat, mt = mts[h]
        mc, mn = mcs[h], mns[h]
        p = jnp.exp(amat - mt)
        qh = q[:, hs].astype(BF16)
        kh = k[:, hs].astype(BF16)
        qk = lax.dot_general(qh, kh, NT, preferred_element_type=F32)
        w = (p * qk).astype(BF16)
        vext = jnp.where(lane_v == HEAD_DIM, 1.0,
                         pb_ref[0, :, GROUP_W + h * LANE:GROUP_W + (h + 1) * LANE].astype(F32)).astype(BF16)
        sth = st[h]
        nd = (jnp.dot(w, vext, preferred_element_type=F32)
              + jnp.exp(mc - mt) * jnp.dot(qh, sth.astype(BF16), preferred_element_type=F32))
        den = nd[:, HEAD_DIM:HEAD_DIM + 1]
        floor = jnp.exp(-(fcol[:, h:h + 1] + mt))
        hh = nd[:, :HEAD_DIM] / jnp.maximum(jnp.abs(den), floor)
        hh = hh * lax.rsqrt(jnp.mean(hh * hh, -1, keepdims=True) + EPS) * hng_ref[:, hs]
        xo = pb_ref[0, :, GROUP_W + N_HEADS * LANE + h * HEAD_DIM:
                    GROUP_W + N_HEADS * LANE + (h + 1) * HEAD_DIM].astype(F32)
        o_ref[0, :, hs] = (_sigmoid(xo) * hh).astype(BF16)
        st[h] = jnp.exp(mc - mn) * sth + jnp.dot(kw_t[hs, :], vext, preferred_element_type=F32)
        mst[:, h:h + 1] = mn


def _mlstm(pb, pg, cw, cb, wq, wk, gb, hng, tri, bsz, seq):
    L = ML_L
    return pl.pallas_call(
        _mlstm_kernel,
        grid=(bsz, seq // L),
        in_specs=[pl.BlockSpec((1, L, PB_W), lambda b, c: (b, c, 0)),
                  pl.BlockSpec((1, L, PG_W), lambda b, c: (b, c, 0)),
                  _full((CONV_B, GROUP_W)), _full((1, GROUP_W)),
                  _full((GROUP_W, GROUP_W)), _full((GROUP_W, GROUP_W)),
                  _full((1, PG_W)), _full((1, GROUP_W)), _full((L, L))],
        out_specs=pl.BlockSpec((1, L, GROUP_W), lambda b, c: (b, c, 0)),
        out_shape=jax.ShapeDtypeStruct((bsz, seq, GROUP_W), BF16),
        scratch_shapes=[pltpu.VMEM((L + ML_HALO, GROUP_W), F32),
                        pltpu.VMEM((N_HEADS, HEAD_DIM, LANE), F32),
                        pltpu.VMEM((1, LANE), F32),
                        pltpu.VMEM((1, LANE), F32)],
        compiler_params=_cparams(("parallel", "arbitrary")),
        name="mlstm",
    )(pb, pg, cw, cb, wq, wk, gb, hng, tri)


SP_STEP = 4


def _spatial_kernel(pc_ref, g_ref, b_ref, ws_ref, bias_ref, o_ref):
    T = SP_CHUNK
    rowi = lax.broadcasted_iota(jnp.int32, (T, N_HEADS * T), 0)
    coli = lax.broadcasted_iota(jnp.int32, (T, N_HEADS * T), 1)
    ws = jnp.where((coli & (T - 1)) <= rowi, ws_ref[...], 0.0).astype(BF16)
    rgrp = lax.broadcasted_iota(jnp.int32, (T, GROUP_W), 1) // HEAD_DIM
    for cidx in range(SP_STEP):
        rs = slice(cidx * T, (cidx + 1) * T)
        u = _gelu(pc_ref[rs, 0:GROUP_W].astype(F32))
        gv = _gelu(pc_ref[rs, GROUP_W:].astype(F32))
        mu = jnp.mean(gv, -1, keepdims=True)
        dv = gv - mu
        var = jnp.mean(dv * dv, -1, keepdims=True)
        vn = (dv * lax.rsqrt(var + EPS) * g_ref[...] + b_ref[...]).astype(BF16)
        zero = jnp.zeros_like(vn)
        vbig = jnp.concatenate([jnp.where(rgrp == g, vn, zero) for g in range(N_HEADS)], axis=0)
        sg = jnp.dot(ws, vbig, preferred_element_type=F32) + bias_ref[...]
        o_ref[rs, :] = (u * sg).astype(BF16)


def _spatial(pc, g, b, ws_cat, bias_full):
    t = pc.shape[0]
    tm = SP_STEP * SP_CHUNK
    return pl.pallas_call(
        _spatial_kernel,
        grid=(t // tm,),
        in_specs=[pl.BlockSpec((tm, PC_W), lambda i: (i, 0)),
                  _full((1, GROUP_W)), _full((1, GROUP_W)),
                  _full((SP_CHUNK, N_HEADS * SP_CHUNK)), _full((SP_CHUNK, GROUP_W))],
        out_specs=pl.BlockSpec((tm, GROUP_W), lambda i: (i, 0)),
        out_shape=jax.ShapeDtypeStruct((t, GROUP_W), BF16),
        compiler_params=_cparams(("parallel",)),
        name="spatial",
    )(pc, g, b, ws_cat, bias_full)


CV_T = 256
CV_HALO = 32


def _convmod_kernel(pd_ref, w_ref, b_ref, avg_ref, g_ref, be_ref, o_ref, ypad):
    i = pl.program_id(1)
    T = CV_T

    @pl.when(i == 0)
    def _init():
        ypad[0:CV_HALO, :] = jnp.zeros((CV_HALO, GROUP_W), F32)

    a = pd_ref[0, :, 0:GROUP_W].astype(F32)
    b = pd_ref[0, :, GROUP_W:].astype(F32)
    y = a * _sigmoid(b)
    ypad[CV_HALO:CV_HALO + T, :] = y
    acc = b_ref[...] + jnp.zeros((T, GROUP_W), F32)
    for j in range(CONV_D):
        off = CV_HALO - (CONV_D - 1) + j
        acc = acc + w_ref[j:j + 1, :] * ypad[off:off + T, :]
    ypad[0:CV_HALO, :] = y[T - CV_HALO:, :]
    mu = _dot_exact_rhs(acc, avg_ref[...])
    dv = acc - mu
    var = _dot_exact_rhs(dv * dv, avg_ref[...])
    yn = dv * lax.rsqrt(var + EPS) * g_ref[...] + be_ref[...]
    o_ref[0] = (yn * _sigmoid(yn)).astype(BF16)


def _convmod(pd, w, b, avg, g, be, bsz, seq):
    return pl.pallas_call(
        _convmod_kernel,
        grid=(bsz, seq // CV_T),
        in_specs=[pl.BlockSpec((1, CV_T, PD_W), lambda bb, i: (bb, i, 0)),
                  _full((CONV_D, GROUP_W)), _full((1, GROUP_W)), _full((GROUP_W, GROUP_W)),
                  _full((1, GROUP_W)), _full((1, GROUP_W))],
        out_specs=pl.BlockSpec((1, CV_T, GROUP_W), lambda bb, i: (bb, i, 0)),
        out_shape=jax.ShapeDtypeStruct((bsz, seq, GROUP_W), BF16),
        scratch_shapes=[pltpu.VMEM((CV_T + CV_HALO, GROUP_W), F32)],
        compiler_params=_cparams(("parallel", "arbitrary")),
        name="convmod",
    )(pd, w, b, avg, g, be)


OP_TM = 512
PQ_W = P_HEADS * P_HALF
KROWS = P_KEYS * P_HEADS


def _outproj_kernel(x_ref, ma_ref, mb_ref, mc_ref, md_ref, wo_ref, g_ref, wpq_ref, k1_ref, k2_ref,
                    k2h_ref, h_ref, xnt_ref, s1_ref, s2_ref, s2h_ref):
    acc = x_ref[...]
    for g, m in enumerate((ma_ref, mb_ref, mc_ref, md_ref)):
        acc = acc + jnp.dot(m[...], wo_ref[g * GROUP_W:(g + 1) * GROUP_W, :], preferred_element_type=F32)
    h_ref[...] = acc
    hn = acc * lax.rsqrt(jnp.mean(acc * acc, -1, keepdims=True) + EPS) * g_ref[...]
    xnt_ref[...] = hn.T.astype(BF16)
    q = jnp.dot(hn.astype(BF16), wpq_ref[...], preferred_element_type=F32).astype(BF16)
    s1_ref[...] = lax.dot_general(k1_ref[...], q[:, :PQ_W], NT, preferred_element_type=F32)
    s2_ref[...] = lax.dot_general(k2_ref[...], q[:, PQ_W:], NT, preferred_element_type=F32)
    for h in range(P_HEADS):
        qh = q[:, PQ_W + h * P_HALF:PQ_W + (h + 1) * P_HALF]
        s2h_ref[h * P_KEYS:(h + 1) * P_KEYS, :] = lax.dot_general(
            k2h_ref[h], qh, NT, preferred_element_type=F32)


def _outproj(x, ma, mb, mc, md, wo, g, wpq, k1, k2, k2h):
    t = x.shape[0]
    tm = OP_TM
    mix = pl.BlockSpec((tm, GROUP_W), lambda i: (i, 0))
    tcol = pl.BlockSpec((KROWS, tm), lambda i: (0, i))
    return pl.pallas_call(
        _outproj_kernel,
        grid=(t // tm,),
        in_specs=[pl.BlockSpec((tm, D_MODEL), lambda i: (i, 0)), mix, mix, mix, mix,
                  _full((D_MODEL, D_MODEL)), _full((1, D_MODEL)), _full((D_MODEL, 2 * PQ_W)),
                  _full((KROWS, PQ_W)), _full((KROWS, PQ_W)), _full((P_HEADS, P_KEYS, P_HALF))],
        out_specs=[pl.BlockSpec((tm, D_MODEL), lambda i: (i, 0)),
                   pl.BlockSpec((D_MODEL, tm), lambda i: (0, i)), tcol, tcol, tcol],
        out_shape=[jax.ShapeDtypeStruct((t, D_MODEL), F32),
                   jax.ShapeDtypeStruct((D_MODEL, t), BF16),
                   jax.ShapeDtypeStruct((KROWS, t), F32),
                   jax.ShapeDtypeStruct((KROWS, t), F32),
                   jax.ShapeDtypeStruct((KROWS, t), F32)],
        compiler_params=_cparams(("parallel",)),
        name="outproj",
    )(x, ma, mb, mc, md, wo, g, wpq, k1, k2, k2h)


RT_T = 256
BF_ROWS = 16


def _bitonic_merge(v):
    n = len(v)
    if n == 1:
        return v
    half = n // 2
    hi = [jnp.maximum(v[i], v[i + half]) for i in range(half)]
    lo = [jnp.minimum(v[i], v[i + half]) for i in range(half)]
    return _bitonic_merge(hi) + _bitonic_merge(lo)


def _bitonic_sort(v):
    n = len(v)
    if n == 1:
        return v
    return _bitonic_merge(_bitonic_sort(v[:n // 2]) + _bitonic_sort(v[n // 2:])[::-1])


def _merge_top(a, b):
    n = len(a)
    c = list(a)
    for k, bv in enumerate(b):
        c[n - 1 - k] = jnp.maximum(a[n - 1 - k], bv)
    return _bitonic_merge(c)


def _sorted_top(ref, lanes):
    top = None
    for grp in range(P_KEYS // P_TOPK):
        tiles = [ref[(grp * P_TOPK + k) * P_HEADS:(grp * P_TOPK + k + 1) * P_HEADS, lanes]
                 for k in range(P_TOPK)]
        srt = _bitonic_sort(tiles)
        top = srt if top is None else _merge_top(top, srt)
    return top


def _route_kernel(s1_ref, s2_ref, s2h_ref, cnt_ref, w_ref, rk_ref, p2h_ref):
    for g in range(RT_T // LANE):
        lanes = slice(g * LANE, (g + 1) * LANE)
        t1 = _sorted_top(s1_ref, lanes)
        t2 = _sorted_top(s2_ref, lanes)
        rows = [[t1[a] + t2[b] for b in range(P_TOPK // (a + 1))] for a in range(P_TOPK)]
        top = rows[0]
        for a in range(1, P_TOPK):
            top = _merge_top(top, rows[a])
        tau = top[P_TOPK - 1]
        cmax = rows[0][0]
        z = jnp.zeros((P_HEADS, LANE), F32)
        cnt_a = []
        for a in range(P_TOPK):
            ca = jnp.zeros((P_HEADS, LANE), F32)
            for cnd in rows[a]:
                sel = cnd >= tau
                z = z + jnp.where(sel, jnp.exp(cnd - cmax), 0.0)
                ca = ca + jnp.where(sel, 1.0, 0.0)
            cnt_a.append(ca)
        zinv = 1.0 / z

        def body(n, carry):
            r0 = pl.multiple_of(n * P_HEADS, P_HEADS)
            s1 = s1_ref[pl.ds(r0, P_HEADS), lanes]
            cn = jnp.zeros((P_HEADS, LANE), F32)
            for a in range(P_TOPK):
                cn = jnp.where(s1 == t1[a], cnt_a[a], cn)
            cnt_ref[pl.ds(r0, P_HEADS), lanes] = cn
            w_ref[pl.ds(r0, P_HEADS), lanes] = jnp.exp(s1 - t1[0]) * zinv
            return carry

        lax.fori_loop(0, P_KEYS, body, 0, unroll=4)

        for h in range(P_HEADS):
            tb = [jnp.broadcast_to(t2[b][h:h + 1, :], (BF_ROWS, LANE)) for b in range(P_TOPK)]

            def rbody(r, carry, h=h, tb=tb):
                r0 = pl.multiple_of(h * P_KEYS + r * BF_ROWS, BF_ROWS)
                s2 = s2h_ref[pl.ds(r0, BF_ROWS), lanes]
                rk = jnp.zeros((BF_ROWS, LANE), F32)
                for b in range(P_TOPK):
                    rk = rk + jnp.where(tb[b] > s2, 1.0, 0.0)
                rk_ref[pl.ds(r0, BF_ROWS), lanes] = rk.astype(BF16)
                p2h_ref[pl.ds(r0, BF_ROWS), lanes] = jnp.exp(s2 - tb[0]).astype(BF16)
                return carry

            lax.fori_loop(0, P_KEYS // BF_ROWS, rbody, 0, unroll=2)


def _route(s1, s2, s2h):
    t = s1.shape[1]
    blk = pl.BlockSpec((KROWS, RT_T), lambda i: (0, i))
    return pl.pallas_call(
        _route_kernel,
        grid=(t // RT_T,),
        in_specs=[blk, blk, blk],
        out_specs=[blk, blk, blk, blk],
        out_shape=[jax.ShapeDtypeStruct((KROWS, t), F32), jax.ShapeDtypeStruct((KROWS, t), F32),
                   jax.ShapeDtypeStruct((KROWS, t), BF16), jax.ShapeDtypeStruct((KROWS, t), BF16)],
        compiler_params=_cparams(("parallel",)),
        name="route",
    )(s1, s2, s2h)


PE_TT = 512
PE_EB = 512
PE_K1 = PE_EB // P_KEYS


PE_GL = 256


def _peer_kernel(xnt_ref, u_ref, vt_ref, cnt_ref, w_ref, rk_ref, p2h_ref, h_ref, o_ref, acc, ht):
    j = pl.program_id(1)

    @pl.when(j == 0)
    def _init():
        acc[...] = jnp.zeros(acc.shape, F32)

    a = jnp.dot(u_ref[...], xnt_ref[...], preferred_element_type=F32)
    ngrp = P_KEYS // BF_ROWS
    for c in range(PE_K1):
        e1 = j * PE_K1 + c
        for s in range(PE_TT // PE_GL):
            ls = slice(s * PE_GL, (s + 1) * PE_GL)
            gate = [jnp.zeros((BF_ROWS, PE_GL), BF16) for _ in range(ngrp)]
            for h in range(P_HEADS):
                cn = jnp.broadcast_to(cnt_ref[e1, h:h + 1, ls], (BF_ROWS, PE_GL)).astype(BF16)
                wg = jnp.broadcast_to(w_ref[e1, h:h + 1, ls], (BF_ROWS, PE_GL)).astype(BF16)
                for r in range(ngrp):
                    rows = slice(h * P_KEYS + r * BF_ROWS, h * P_KEYS + (r + 1) * BF_ROWS)
                    sel = jnp.where(rk_ref[rows, ls] < cn, p2h_ref[rows, ls], jnp.zeros((), BF16))
                    gate[r] = gate[r] + sel * wg
            for r in range(ngrp):
                rows = slice(c * P_KEYS + r * BF_ROWS, c * P_KEYS + (r + 1) * BF_ROWS)
                ht[rows, ls] = _gelu(a[rows, ls]).astype(BF16) * gate[r]
    acc[...] += jnp.dot(vt_ref[...], ht[...], preferred_element_type=F32)

    @pl.when(j == pl.num_programs(1) - 1)
    def _fin():
        o_ref[...] = h_ref[...] + acc[...].T


def _peer(xnt, u, vt, thr, w, s2h, p2h, hres):
    t = xnt.shape[1]
    tok = lambda r: pl.BlockSpec((r, PE_TT), lambda i, j: (0, i))
    tok3 = pl.BlockSpec((P_KEYS, P_HEADS, PE_TT), lambda i, j: (0, 0, i))
    return pl.pallas_call(
        _peer_kernel,
        grid=(t // PE_TT, N_EXPERTS // PE_EB),
        in_specs=[tok(D_MODEL),
                  pl.BlockSpec((PE_EB, D_MODEL), lambda i, j: (j, 0)),
                  pl.BlockSpec((D_MODEL, PE_EB), lambda i, j: (0, j)),
                  tok3, tok3, tok(KROWS), tok(KROWS),
                  pl.BlockSpec((PE_TT, D_MODEL), lambda i, j: (i, 0))],
        out_specs=pl.BlockSpec((PE_TT, D_MODEL), lambda i, j: (i, 0)),
        out_shape=jax.ShapeDtypeStruct((t, D_MODEL), F32),
        scratch_shapes=[pltpu.VMEM((D_MODEL, PE_TT), F32), pltpu.VMEM((PE_EB, PE_TT), BF16)],
        compiler_params=_cparams(("parallel", "arbitrary")),
        name="peer",
    )(xnt, u, vt, thr, w, s2h, p2h, hres)


def _pad_cols(w, width):
    return jnp.pad(w, ((0, 0), (0, width - w.shape[1])))


def _prep_layer(l, p):
    w_in = p["w_in"][l]
    o = 0
    cuts = {}
    for name, wdt in (("a", Q_RANK + KV_RANK + ROPE), ("bqk", GROUP_W), ("bv", GROUP_W), ("bo", GROUP_W),
                      ("bi", N_HEADS), ("bf", N_HEADS), ("c", 2 * GROUP_W), ("d", 2 * GROUP_W)):
        cuts[name] = w_in[:, o:o + wdt]
        o += wdt
    bv = cuts["bv"].reshape(D_MODEL, N_HEADS, HEAD_DIM)
    bv = jnp.pad(bv, ((0, 0), (0, 0), (0, LANE - HEAD_DIM))).reshape(D_MODEL, N_HEADS * LANE)
    w_all = jnp.concatenate([
        _pad_cols(cuts["a"], PA_W), cuts["bqk"], bv, cuts["bo"],
        _pad_cols(cuts["bi"], LANE), _pad_cols(cuts["bf"], LANE), cuts["c"], cuts["d"]], axis=1).astype(BF16)

    def head_pad(w, width):
        return jnp.pad(w, ((0, 0), (0, 0), (0, LANE - width))).reshape(w.shape[0], N_HEADS * LANE)

    wuq = head_pad(p["a_w_uq"][l].reshape(Q_RANK, N_HEADS, QK_DIM), QK_DIM).astype(BF16)
    wukv = p["a_w_ukv"][l].reshape(KV_RANK, N_HEADS, NOPE + HEAD_DIM)
    wk = head_pad(wukv[:, :, :NOPE], NOPE).astype(BF16)
    wv = head_pad(wukv[:, :, NOPE:], HEAD_DIM).astype(BF16)
    qg = _pad_cols(p["a_qn_g"][l][None, :] * (QK_DIM ** -0.5), LANE)
    kg = _pad_cols(p["a_kn_g"][l][None, :], LANE)

    eye = jnp.eye(N_HEADS, dtype=F32)
    wq_bd = jnp.einsum("hde,hg->hdge", p["b_w_q"][l], eye).reshape(GROUP_W, GROUP_W).astype(BF16)
    wk_bd = (jnp.einsum("hde,hg->hdge", p["b_w_k"][l], eye).reshape(GROUP_W, GROUP_W)
             * (HEAD_DIM ** -0.5)).astype(BF16)
    gb = jnp.concatenate([_pad_cols(p["b_b_i"][l][None, :], LANE), _pad_cols(p["b_b_f"][l][None, :], LANE)], axis=1)

    ws_cat = p["c_w_s"][l].transpose(1, 0, 2).reshape(SP_CHUNK, N_HEADS * SP_CHUNK)
    bias_full = jnp.repeat(p["c_b_s"][l].T, HEAD_DIM, axis=1)

    wpq = p["p_w_q"][l].reshape(D_MODEL, P_HEADS, 2, P_HALF).transpose(0, 2, 1, 3).reshape(D_MODEL, 2 * PQ_W)
    keys = p["p_sub_keys"][l]
    eye8 = jnp.eye(P_HEADS, dtype=F32)
    kmat = [jnp.einsum("hnd,hg->nhgd", keys[:, s], eye8).reshape(KROWS, PQ_W).astype(BF16) for s in range(2)]
    return dict(
        norm1_g=p["norm1_g"][l][None, :], w_all=w_all,
        cqg=p["a_cq_g"][l][None, :], ckvg=p["a_ckv_g"][l][None, :], wuq=wuq, wk=wk, wv=wv, qg=qg, kg=kg,
        b_cw=p["b_conv_w"][l], b_cb=p["b_conv_b"][l][None, :], wq_bd=wq_bd, wk_bd=wk_bd, gb=gb,
        hng=p["b_hn_g"][l].reshape(1, GROUP_W),
        c_g=p["c_ln_g"][l][None, :], c_b=p["c_ln_b"][l][None, :], ws_cat=ws_cat, bias_full=bias_full,
        d_w=p["d_dw_w"][l], d_b=p["d_dw_b"][l][None, :],
        d_g=p["d_cn_g"][l].reshape(1, GROUP_W), d_be=p["d_cn_b"][l].reshape(1, GROUP_W),
        w_out=p["w_out"][l].astype(BF16), norm2_g=p["norm2_g"][l][None, :], wpq=wpq.astype(BF16),
        k1=kmat[0], k2=kmat[1], k2h=keys[:, 1].astype(BF16),
        u=p["p_u"][l].astype(BF16), vt=p["p_v"][l].T.astype(BF16),
    )


def _layer(x, pos3, consts, w, bsz, seq):
    t = bsz * seq
    pa, pb, pg, pc, pd = _inproj(x, w["norm1_g"], w["w_all"])
    r3 = lambda a: a.reshape(bsz, seq, a.shape[-1])
    mix_a = _mla(r3(pa), pos3, consts["invf"], w["cqg"], w["ckvg"], w["wuq"], w["wk"], w["wv"],
                 w["qg"], w["kg"], bsz, seq).reshape(t, GROUP_W)
    mix_b = _mlstm(r3(pb), r3(pg), w["b_cw"], w["b_cb"], w["wq_bd"], w["wk_bd"], w["gb"], w["hng"],
                   consts["tri"], bsz, seq).reshape(t, GROUP_W)
    mix_c = _spatial(pc, w["c_g"], w["c_b"], w["ws_cat"], w["bias_full"])
    mix_d = _convmod(r3(pd), w["d_w"], w["d_b"], consts["avg"], w["d_g"], w["d_be"], bsz, seq).reshape(t, GROUP_W)
    h, xnt, s1, s2, s2h = _outproj(x, mix_a, mix_b, mix_c, mix_d, w["w_out"], w["norm2_g"], w["wpq"],
                                   w["k1"], w["k2"], w["k2h"])
    cnt, wgt, rk, p2h = _route(s1, s2, s2h)
    shp = (P_KEYS, P_HEADS, t)
    return _peer(xnt, w["u"], w["vt"], cnt.reshape(shp), wgt.reshape(shp), rk, p2h, h)


def _consts():
    half = ROPE // 2
    inv_freq = ROPE_BASE ** (-jnp.arange(0, ROPE, 2, dtype=F32) / ROPE)
    invf = jnp.zeros((1, LANE), F32)
    invf = invf.at[0, NOPE:NOPE + half].set(inv_freq).at[0, NOPE + half:QK_DIM].set(inv_freq)
    tri = jnp.tril(jnp.ones((ML_L, ML_L), F32)).astype(BF16)
    grp = jnp.arange(GROUP_W) // HEAD_DIM
    avg = ((grp[:, None] == grp[None, :]).astype(F32) / HEAD_DIM).astype(BF16)
    return dict(invf=invf, tri=tri, avg=avg)


def kernel(x, positions, norm1_g, w_in, a_cq_g, a_ckv_g, a_w_uq, a_w_ukv, a_qn_g, a_kn_g, b_conv_w, b_conv_b, b_w_q, b_w_k, b_b_i, b_b_f, b_hn_g, c_ln_g, c_ln_b, c_w_s, c_b_s, d_dw_w, d_dw_b, d_cn_g, d_cn_b, w_out, norm2_g, p_w_q, p_sub_keys, p_u, p_v):
    params = dict(norm1_g=norm1_g, w_in=w_in, a_cq_g=a_cq_g, a_ckv_g=a_ckv_g, a_w_uq=a_w_uq, a_w_ukv=a_w_ukv,
                  a_qn_g=a_qn_g, a_kn_g=a_kn_g, b_conv_w=b_conv_w, b_conv_b=b_conv_b, b_w_q=b_w_q, b_w_k=b_w_k,
                  b_b_i=b_b_i, b_b_f=b_b_f, b_hn_g=b_hn_g, c_ln_g=c_ln_g, c_ln_b=c_ln_b, c_w_s=c_w_s,
                  c_b_s=c_b_s, d_dw_w=d_dw_w, d_dw_b=d_dw_b, d_cn_g=d_cn_g, d_cn_b=d_cn_b, w_out=w_out,
                  norm2_g=norm2_g, p_w_q=p_w_q, p_sub_keys=p_sub_keys, p_u=p_u, p_v=p_v)
    bsz, seq, d = x.shape
    consts = _consts()
    pos3 = positions.reshape(bsz, seq, 1)
    xf = x.reshape(bsz * seq, d)
    for l in range(norm1_g.shape[0]):
        xf = _layer(xf, pos3, consts, _prep_layer(l, params), bsz, seq)
    return xf.reshape(bsz, seq, d)
```

```python
import functools

import jax
import jax.numpy as jnp
from jax import lax
from jax.experimental import pallas as pl
from jax.experimental.pallas import tpu as pltpu

F32 = jnp.float32
BF16 = jnp.bfloat16

D_MODEL = 1024
N_HEADS = 4
HEAD_DIM = 64
GROUP_W = N_HEADS * HEAD_DIM
Q_RANK, KV_RANK = 256, 128
NOPE, ROPE = 64, 32
QK_DIM = NOPE + ROPE
ROPE_BASE = 10000.0
LANE = 128
CONV_B = 4
CONV_D = 31
SP_CHUNK = 128
P_HEADS = 8
P_KEYS = 128
P_TOPK = 16
P_HALF = 64
N_EXPERTS = P_KEYS * P_KEYS
EPS = 1e-6
NEG = -1e30
VMEM_LIMIT = 56 * 1024 * 1024

NT = (((1,), (1,)), ((), ()))


def _cparams(sem, flags=None):
    return pltpu.CompilerParams(dimension_semantics=sem, vmem_limit_bytes=VMEM_LIMIT, flags=flags)


def _full(shape):
    n = len(shape)
    return pl.BlockSpec(shape, lambda *_: (0,) * n)


def _split3(a):
    hi = a.astype(BF16)
    r1 = a - hi.astype(F32)
    mid = r1.astype(BF16)
    lo = (r1 - mid.astype(F32)).astype(BF16)
    return hi, mid, lo


def _dot_exact_rhs(a, m_bf16):
    hi, mid, lo = _split3(a)
    d = lambda t: jnp.dot(t, m_bf16, preferred_element_type=F32)
    return d(hi) + d(mid) + d(lo)


def _dot_exact_lhs(m_bf16, a):
    hi, mid, lo = _split3(a)
    d = lambda t: jnp.dot(m_bf16, t, preferred_element_type=F32)
    return d(hi) + d(mid) + d(lo)


def _sigmoid(x):
    return 1.0 / (1.0 + jnp.exp(-x))


def _gelu(x):
    return 0.5 * x * (1.0 + jnp.tanh(0.7978845608028654 * (x + 0.044715 * (x * x * x))))


PA_W, PB_W, PG_W, PC_W, PD_W = 512, 1024, 256, 512, 512
IN_COLS = (0, PA_W, PA_W + PB_W, PA_W + PB_W + PG_W, PA_W + PB_W + PG_W + PC_W,
           PA_W + PB_W + PG_W + PC_W + PD_W)


def _inproj_kernel(x_ref, g_ref, w_ref, pa_ref, pb_ref, pg_ref, pc_ref, pd_ref):
    x = x_ref[...]
    hn = (x * lax.rsqrt(jnp.mean(x * x, axis=-1, keepdims=True) + EPS) * g_ref[...]).astype(BF16)
    outs = (pa_ref, pb_ref, pg_ref, pc_ref, pd_ref)
    for k, o in enumerate(outs):
        y = jnp.dot(hn, w_ref[:, IN_COLS[k]:IN_COLS[k + 1]], preferred_element_type=F32)
        o[...] = y.astype(o.dtype)


def _inproj(x, g, w_all, tm=512):
    t = x.shape[0]
    widths = (PA_W, PB_W, PG_W, PC_W, PD_W)
    dts = (BF16, BF16, F32, BF16, BF16)
    return pl.pallas_call(
        _inproj_kernel,
        grid=(t // tm,),
        in_specs=[pl.BlockSpec((tm, D_MODEL), lambda i: (i, 0)),
                  _full((1, D_MODEL)), _full((D_MODEL, IN_COLS[-1]))],
        out_specs=[pl.BlockSpec((tm, w), lambda i: (i, 0)) for w in widths],
        out_shape=[jax.ShapeDtypeStruct((t, w), d) for w, d in zip(widths, dts)],
        compiler_params=_cparams(("parallel",)),
        name="inproj",
    )(x, g, w_all)


MLA_TQ = 256
MLA_PRO = 512


def _rope_kernel(pos_ref, invf_ref, cos_ref, sin_ref):
    ang = pos_ref[...].astype(F32) * invf_ref[...]
    cos_ref[...] = jnp.cos(ang)
    sin_ref[...] = jnp.sin(ang)


def _rope_tables(pos, invf, tm=1024):
    t = pos.shape[0]
    blk = pl.BlockSpec((tm, LANE), lambda i: (i, 0))
    return pl.pallas_call(
        _rope_kernel,
        grid=(t // tm,),
        in_specs=[pl.BlockSpec((tm, 1), lambda i: (i, 0)), _full((1, LANE))],
        out_specs=[blk, blk],
        out_shape=[jax.ShapeDtypeStruct((t, LANE), F32)] * 2,
        compiler_params=_cparams(("parallel",)),
        name="rope",
    )(pos, invf)


def _mla_kernel(pa_ref, cos_ref, sin_ref, cqg_ref, ckvg_ref, wuq_ref, wk_ref, wv_ref,
                qg_ref, kg_ref, o_ref, q_s, k_s, v_s, *, seq):
    tq = MLA_TQ

    def chunk(c, carry):
        r0 = pl.multiple_of(c * MLA_PRO, MLA_PRO)
        pa = pa_ref[0, pl.ds(r0, MLA_PRO), :]
        cq = pa[:, 0:Q_RANK].astype(F32)
        ckv = pa[:, Q_RANK:Q_RANK + KV_RANK].astype(F32)
        krp = pa[:, Q_RANK + KV_RANK:].astype(F32)
        cqn = (cq * lax.rsqrt(jnp.mean(cq * cq, -1, keepdims=True) + EPS) * cqg_ref[...]).astype(BF16)
        ckvn = (ckv * lax.rsqrt(jnp.mean(ckv * ckv, -1, keepdims=True) + EPS) * ckvg_ref[...]).astype(BF16)
        q = jnp.dot(cqn, wuq_ref[...], preferred_element_type=F32)
        kn = jnp.dot(ckvn, wk_ref[...], preferred_element_type=F32)
        v = jnp.dot(ckvn, wv_ref[...], preferred_element_type=F32)
        lane = lax.broadcasted_iota(jnp.int32, (MLA_PRO, LANE), 1)
        cosv = cos_ref[0, pl.ds(r0, MLA_PRO), :]
        sinv = sin_ref[0, pl.ds(r0, MLA_PRO), :]
        half = ROPE // 2
        sin_hi = jnp.where((lane >= NOPE + half) & (lane < QK_DIM), sinv, 0.0)
        sin_lo = jnp.where((lane >= NOPE) & (lane < NOPE + half), -sinv, 0.0)
        kr_sh = pltpu.roll(krp, NOPE, 1)

        def norm_rope(t, g):
            t = t * lax.rsqrt(jnp.sum(t * t, -1, keepdims=True) * (1.0 / QK_DIM) + EPS) * g
            return (t * cosv + pltpu.roll(t, half, 1) * sin_hi
                    + pltpu.roll(t, LANE - half, 1) * sin_lo)

        for h in range(N_HEADS):
            sl = slice(h * LANE, (h + 1) * LANE)
            q_s[pl.ds(r0, MLA_PRO), sl] = norm_rope(q[:, sl], qg_ref[...]).astype(BF16)
            k_s[pl.ds(r0, MLA_PRO), sl] = norm_rope(kn[:, sl] + kr_sh, kg_ref[...]).astype(BF16)
            v_s[pl.ds(r0, MLA_PRO), sl] = jnp.where(lane == HEAD_DIM, 1.0, v[:, sl]).astype(BF16)
        return carry

    lax.fori_loop(0, seq // MLA_PRO, chunk, 0)

    row = lax.broadcasted_iota(jnp.int32, (tq, tq), 0)
    col = lax.broadcasted_iota(jnp.int32, (tq, tq), 1)
    for i in range(seq // tq):
        klen = (i + 1) * tq
        for h in range(N_HEADS):
            sl = slice(h * LANE, (h + 1) * LANE)
            qh = q_s[i * tq:(i + 1) * tq, sl]
            s = lax.dot_general(qh, k_s[0:klen, sl], NT, preferred_element_type=F32)
            diag = jnp.where(col <= row, s[:, klen - tq:], NEG)
            s = diag if i == 0 else jnp.concatenate([s[:, :klen - tq], diag], axis=1)
            p = jnp.exp(s - jnp.max(s, -1, keepdims=True)).astype(BF16)
            acc = jnp.dot(p, v_s[0:klen, sl], preferred_element_type=F32)
            out_h = acc[:, :HEAD_DIM] / acc[:, HEAD_DIM:HEAD_DIM + 1]
            o_ref[0, i * tq:(i + 1) * tq, h * HEAD_DIM:(h + 1) * HEAD_DIM] = out_h.astype(BF16)


def _mla(pa, cosv, sinv, cqg, ckvg, wuq, wk, wv, qg, kg, bsz, seq):
    hw = N_HEADS * LANE
    seq_blk = lambda w: pl.BlockSpec((1, seq, w), lambda b: (b, 0, 0))
    return pl.pallas_call(
        functools.partial(_mla_kernel, seq=seq),
        grid=(bsz,),
        in_specs=[seq_blk(PA_W), seq_blk(LANE), seq_blk(LANE),
                  _full((1, Q_RANK)), _full((1, KV_RANK)),
                  _full((Q_RANK, hw)), _full((KV_RANK, hw)), _full((KV_RANK, hw)),
                  _full((1, LANE)), _full((1, LANE))],
        out_specs=seq_blk(GROUP_W),
        out_shape=jax.ShapeDtypeStruct((bsz, seq, GROUP_W), BF16),
        scratch_shapes=[pltpu.VMEM((seq, hw), BF16)] * 3,
        compiler_params=_cparams(("parallel",)),
        name="mla",
    )(pa, cosv, sinv, cqg, ckvg, wuq, wk, wv, qg, kg)


ML_L = 128
ML_HALO = 8


def _mlstm_kernel(pb_ref, pg_ref, cw_ref, cb_ref, wq_ref, wk_ref, gb_ref, hng_ref, tri_ref,
                  o_ref, xpad, st, mst, fst):
    c = pl.program_id(1)
    L = ML_L

    @pl.when(c == 0)
    def _init():
        xpad[0:ML_HALO, :] = jnp.zeros((ML_HALO, GROUP_W), F32)
        st[...] = jnp.zeros(st.shape, F32)
        mst[...] = jnp.full(mst.shape, NEG, F32)
        fst[...] = jnp.zeros(fst.shape, F32)

    xqk = pb_ref[0, :, 0:GROUP_W].astype(F32)
    xpad[ML_HALO:ML_HALO + L, :] = xqk
    acc = cb_ref[...] + jnp.zeros((L, GROUP_W), F32)
    for j in range(CONV_B):
        off = ML_HALO - (CONV_B - 1) + j
        acc = acc + cw_ref[j:j + 1, :] * xpad[off:off + L, :]
    xpad[0:ML_HALO, :] = xqk[L - ML_HALO:, :]
    xc = (acc * _sigmoid(acc)).astype(BF16)
    q = jnp.dot(xc, wq_ref[...], preferred_element_type=F32)
    k = jnp.dot(xc, wk_ref[...], preferred_element_type=F32)

    gates = pg_ref[0] + gb_ref[...]
    ig = gates[:, :LANE]
    fg = gates[:, LANE:]
    lf = jnp.minimum(fg, 0.0) - jnp.log(1.0 + jnp.exp(-jnp.abs(fg)))
    fcol = _dot_exact_lhs(tri_ref[...], lf) + fst[...]
    fst[...] = fcol[L - 1:L, :]
    a_col = ig - fcol
    a_t = a_col.T

    rowi = lax.broadcasted_iota(jnp.int32, (L, L), 0)
    coli = lax.broadcasted_iota(jnp.int32, (L, L), 1)
    causal = coli <= rowi
    lane_v = lax.broadcasted_iota(jnp.int32, (L, LANE), 1)
    lane_w = lax.broadcasted_iota(jnp.int32, (L, GROUP_W), 1)

    mts, mns, mcs = [], [], []
    for h in range(N_HEADS):
        amat = jnp.where(causal, a_t[h:h + 1, :], NEG)
        mc = mst[:, h:h + 1]
        mt = jnp.maximum(jnp.max(amat, -1, keepdims=True), mc)
        mts.append((amat, mt))
        mcs.append(mc)
        mns.append(mt[L - 1:L, :])

    wg = jnp.exp(a_col[:, N_HEADS - 1:N_HEADS] - mns[N_HEADS - 1])
    for h in range(N_HEADS - 2, -1, -1):
        wg = jnp.where(lane_w < (h + 1) * HEAD_DIM, jnp.exp(a_col[:, h:h + 1] - mns[h]), wg)
    kw_t = (k * wg).T.astype(BF16)

    for h in range(N_HEADS):
        hs = slice(h * HEAD_DIM, (h + 1) * HEAD_DIM)
        amat, mt = mts[h]
        mc, mn = mcs[h], mns[h]
        p = jnp.exp(amat - mt)
        qh = q[:, hs].astype(BF16)
        kh = k[:, hs].astype(BF16)
        qk = lax.dot_general(qh, kh, NT, preferred_element_type=F32)
        w = (p * qk).astype(BF16)
        vext = jnp.where(lane_v == HEAD_DIM, 1.0,
                         pb_ref[0, :, GROUP_W + h * LANE:GROUP_W + (h + 1) * LANE].astype(F32)).astype(BF16)
        sth = st[h]
        nd = (jnp.dot(w, vext, preferred_element_type=F32)
              + jnp.exp(mc - mt) * jnp.dot(qh, sth.astype(BF16), preferred_element_type=F32))
        den = nd[:, HEAD_DIM:HEAD_DIM + 1]
        floor = jnp.exp(-(fcol[:, h:h + 1] + mt))
        hh = nd[:, :HEAD_DIM] / jnp.maximum(jnp.abs(den), floor)
        hh = hh * lax.rsqrt(jnp.mean(hh * hh, -1, keepdims=True) + EPS) * hng_ref[:, hs]
        xo = pb_ref[0, :, GROUP_W + N_HEADS * LANE + h * HEAD_DIM:
                    GROUP_W + N_HEADS * LANE + (h + 1) * HEAD_DIM].astype(F32)
        o_ref[0, :, hs] = (_sigmoid(xo) * hh).astype(BF16)
        st[h] = jnp.exp(mc - mn) * sth + jnp.dot(kw_t[hs, :], vext, preferred_element_type=F32)
        mst[:, h:h + 1] = mn


def _mlstm(pb, pg, cw, cb, wq, wk, gb, hng, tri, bsz, seq):
    L = ML_L
    return pl.pallas_call(
        _mlstm_kernel,
        grid=(bsz, seq // L),
        in_specs=[pl.BlockSpec((1, L, PB_W), lambda b, c: (b, c, 0)),
                  pl.BlockSpec((1, L, PG_W), lambda b, c: (b, c, 0)),
                  _full((CONV_B, GROUP_W)), _full((1, GROUP_W)),
                  _full((GROUP_W, GROUP_W)), _full((GROUP_W, GROUP_W)),
                  _full((1, PG_W)), _full((1, GROUP_W)), _full((L, L))],
        out_specs=pl.BlockSpec((1, L, GROUP_W), lambda b, c: (b, c, 0)),
        out_shape=jax.ShapeDtypeStruct((bsz, seq, GROUP_W), BF16),
        scratch_shapes=[pltpu.VMEM((L + ML_HALO, GROUP_W), F32),
                        pltpu.VMEM((N_HEADS, HEAD_DIM, LANE), F32),
                        pltpu.VMEM((1, LANE), F32),
                        pltpu.VMEM((1, LANE), F32)],
        compiler_params=_cparams(("parallel", "arbitrary")),
        name="mlstm",
    )(pb, pg, cw, cb, wq, wk, gb, hng, tri)


SP_STEP = 4


def _spatial_kernel(pc_ref, g_ref, b_ref, ws_ref, bias_ref, o_ref):
    T = SP_CHUNK
    rowi = lax.broadcasted_iota(jnp.int32, (T, N_HEADS * T), 0)
    coli = lax.broadcasted_iota(jnp.int32, (T, N_HEADS * T), 1)
    ws = jnp.where((coli & (T - 1)) <= rowi, ws_ref[...], 0.0).astype(BF16)
    rgrp = lax.broadcasted_iota(jnp.int32, (T, GROUP_W), 1) // HEAD_DIM
    for cidx in range(SP_STEP):
        rs = slice(cidx * T, (cidx + 1) * T)
        u = _gelu(pc_ref[rs, 0:GROUP_W].astype(F32))
        gv = _gelu(pc_ref[rs, GROUP_W:].astype(F32))
        mu = jnp.mean(gv, -1, keepdims=True)
        dv = gv - mu
        var = jnp.mean(dv * dv, -1, keepdims=True)
        vn = (dv * lax.rsqrt(var + EPS) * g_ref[...] + b_ref[...]).astype(BF16)
        zero = jnp.zeros_like(vn)
        vbig = jnp.concatenate([jnp.where(rgrp == g, vn, zero) for g in range(N_HEADS)], axis=0)
        sg = jnp.dot(ws, vbig, preferred_element_type=F32) + bias_ref[...]
        o_ref[rs, :] = (u * sg).astype(BF16)


def _spatial(pc, g, b, ws_cat, bias_full):
    t = pc.shape[0]
    tm = SP_STEP * SP_CHUNK
    return pl.pallas_call(
        _spatial_kernel,
        grid=(t // tm,),
        in_specs=[pl.BlockSpec((tm, PC_W), lambda i: (i, 0)),
                  _full((1, GROUP_W)), _full((1, GROUP_W)),
                  _full((SP_CHUNK, N_HEADS * SP_CHUNK)), _full((SP_CHUNK, GROUP_W))],
        out_specs=pl.BlockSpec((tm, GROUP_W), lambda i: (i, 0)),
        out_shape=jax.ShapeDtypeStruct((t, GROUP_W), BF16),
        compiler_params=_cparams(("parallel",)),
        name="spatial",
    )(pc, g, b, ws_cat, bias_full)


CV_T = 256
CV_HALO = 32


def _convmod_kernel(pd_ref, w_ref, b_ref, avg_ref, g_ref, be_ref, o_ref, ypad):
    i = pl.program_id(1)
    T = CV_T

    @pl.when(i == 0)
    def _init():
        ypad[0:CV_HALO, :] = jnp.zeros((CV_HALO, GROUP_W), F32)

    a = pd_ref[0, :, 0:GROUP_W].astype(F32)
    b = pd_ref[0, :, GROUP_W:].astype(F32)
    y = a * _sigmoid(b)
    ypad[CV_HALO:CV_HALO + T, :] = y
    acc = b_ref[...] + jnp.zeros((T, GROUP_W), F32)
    for j in range(CONV_D):
        off = CV_HALO - (CONV_D - 1) + j
        acc = acc + w_ref[j:j + 1, :] * ypad[off:off + T, :]
    ypad[0:CV_HALO, :] = y[T - CV_HALO:, :]
    mu = _dot_exact_rhs(acc, avg_ref[...])
    dv = acc - mu
    var = _dot_exact_rhs(dv * dv, avg_ref[...])
    yn = dv * lax.rsqrt(var + EPS) * g_ref[...] + be_ref[...]
    o_ref[0] = (yn * _sigmoid(yn)).astype(BF16)


def _convmod(pd, w, b, avg, g, be, bsz, seq):
    return pl.pallas_call(
        _convmod_kernel,
        grid=(bsz, seq // CV_T),
        in_specs=[pl.BlockSpec((1, CV_T, PD_W), lambda bb, i: (bb, i, 0)),
                  _full((CONV_D, GROUP_W)), _full((1, GROUP_W)), _full((GROUP_W, GROUP_W)),
                  _full((1, GROUP_W)), _full((1, GROUP_W))],
        out_specs=pl.BlockSpec((1, CV_T, GROUP_W), lambda bb, i: (bb, i, 0)),
        out_shape=jax.ShapeDtypeStruct((bsz, seq, GROUP_W), BF16),
        scratch_shapes=[pltpu.VMEM((CV_T + CV_HALO, GROUP_W), F32)],
        compiler_params=_cparams(("parallel", "arbitrary")),
        name="convmod",
    )(pd, w, b, avg, g, be)


OP_TM = 512
PQ_W = P_HEADS * P_HALF
KROWS = P_KEYS * P_HEADS


def _outproj_kernel(x_ref, ma_ref, mb_ref, mc_ref, md_ref, wo_ref, g_ref, wpq_ref, k1_ref, k2_ref,
                    k2h_ref, h_ref, xnt_ref, s1_ref, s2_ref, s2h_ref):
    acc = x_ref[...]
    for g, m in enumerate((ma_ref, mb_ref, mc_ref, md_ref)):
        acc = acc + jnp.dot(m[...], wo_ref[g * GROUP_W:(g + 1) * GROUP_W, :], preferred_element_type=F32)
    h_ref[...] = acc
    hn = acc * lax.rsqrt(jnp.mean(acc * acc, -1, keepdims=True) + EPS) * g_ref[...]
    xnt_ref[...] = hn.T.astype(BF16)
    q = jnp.dot(hn.astype(BF16), wpq_ref[...], preferred_element_type=F32).astype(BF16)
    s1_ref[...] = lax.dot_general(k1_ref[...], q[:, :PQ_W], NT, preferred_element_type=F32)
    s2_ref[...] = lax.dot_general(k2_ref[...], q[:, PQ_W:], NT, preferred_element_type=F32)
    for h in range(P_HEADS):
        qh = q[:, PQ_W + h * P_HALF:PQ_W + (h + 1) * P_HALF]
        s2h_ref[h * P_KEYS:(h + 1) * P_KEYS, :] = lax.dot_general(
            k2h_ref[h], qh, NT, preferred_element_type=F32)


def _outproj(x, ma, mb, mc, md, wo, g, wpq, k1, k2, k2h):
    t = x.shape[0]
    tm = OP_TM
    mix = pl.BlockSpec((tm, GROUP_W), lambda i: (i, 0))
    tcol = pl.BlockSpec((KROWS, tm), lambda i: (0, i))
    return pl.pallas_call(
        _outproj_kernel,
        grid=(t // tm,),
        in_specs=[pl.BlockSpec((tm, D_MODEL), lambda i: (i, 0)), mix, mix, mix, mix,
                  _full((D_MODEL, D_MODEL)), _full((1, D_MODEL)), _full((D_MODEL, 2 * PQ_W)),
                  _full((KROWS, PQ_W)), _full((KROWS, PQ_W)), _full((P_HEADS, P_KEYS, P_HALF))],
        out_specs=[pl.BlockSpec((tm, D_MODEL), lambda i: (i, 0)),
                   pl.BlockSpec((D_MODEL, tm), lambda i: (0, i)), tcol, tcol, tcol],
        out_shape=[jax.ShapeDtypeStruct((t, D_MODEL), F32),
                   jax.ShapeDtypeStruct((D_MODEL, t), BF16),
                   jax.ShapeDtypeStruct((KROWS, t), F32),
                   jax.ShapeDtypeStruct((KROWS, t), F32),
                   jax.ShapeDtypeStruct((KROWS, t), F32)],
        compiler_params=_cparams(("parallel",)),
        name="outproj",
    )(x, ma, mb, mc, md, wo, g, wpq, k1, k2, k2h)


RT_T = 256
BF_ROWS = 16


def _bitonic_merge(v):
    n = len(v)
    if n == 1:
        return v
    half = n // 2
    hi = [jnp.maximum(v[i], v[i + half]) for i in range(half)]
    lo = [jnp.minimum(v[i], v[i + half]) for i in range(half)]
    return _bitonic_merge(hi) + _bitonic_merge(lo)


def _bitonic_sort(v):
    n = len(v)
    if n == 1:
        return v
    return _bitonic_merge(_bitonic_sort(v[:n // 2]) + _bitonic_sort(v[n // 2:])[::-1])


def _merge_top(a, b):
    n = len(a)
    c = list(a)
    for k, bv in enumerate(b):
        c[n - 1 - k] = jnp.maximum(a[n - 1 - k], bv)
    return _bitonic_merge(c)


def _sorted_top(ref, lanes):
    top = None
    for grp in range(P_KEYS // P_TOPK):
        tiles = [ref[(grp * P_TOPK + k) * P_HEADS:(grp * P_TOPK + k + 1) * P_HEADS, lanes]
                 for k in range(P_TOPK)]
        srt = _bitonic_sort(tiles)
        top = srt if top is None else _merge_top(top, srt)
    return top


def _route_kernel(s1_ref, s2_ref, s2h_ref, cnt_ref, w_ref, rk_ref, p2h_ref):
    for g in range(RT_T // LANE):
        lanes = slice(g * LANE, (g + 1) * LANE)
        t1 = _sorted_top(s1_ref, lanes)
        t2 = _sorted_top(s2_ref, lanes)
        rows = [[t1[a] + t2[b] for b in range(P_TOPK // (a + 1))] for a in range(P_TOPK)]
        top = rows[0]
        for a in range(1, P_TOPK):
            top = _merge_top(top, rows[a])
        tau = top[P_TOPK - 1]
        cmax = rows[0][0]
        z = jnp.zeros((P_HEADS, LANE), F32)
        cnt_a = []
        for a in range(P_TOPK):
            ca = jnp.zeros((P_HEADS, LANE), F32)
            for cnd in rows[a]:
                sel = cnd >= tau
                z = z + jnp.where(sel, jnp.exp(cnd - cmax), 0.0)
                ca = ca + jnp.where(sel, 1.0, 0.0)
            cnt_a.append(ca)
        zinv = 1.0 / z

        def body(n, carry):
            r0 = pl.multiple_of(n * P_HEADS, P_HEADS)
            s1 = s1_ref[pl.ds(r0, P_HEADS), lanes]
            cn = jnp.zeros((P_HEADS, LANE), F32)
            for a in range(P_TOPK):
                cn = jnp.where(s1 == t1[a], cnt_a[a], cn)
            cnt_ref[pl.ds(r0, P_HEADS), lanes] = cn
            w_ref[pl.ds(r0, P_HEADS), lanes] = jnp.exp(s1 - t1[0]) * zinv
            return carry

        lax.fori_loop(0, P_KEYS, body, 0, unroll=4)

        for h in range(P_HEADS):
            tb = [jnp.broadcast_to(t2[b][h:h + 1, :], (BF_ROWS, LANE)) for b in range(P_TOPK)]

            def rbody(r, carry, h=h, tb=tb):
                r0 = pl.multiple_of(h * P_KEYS + r * BF_ROWS, BF_ROWS)
                s2 = s2h_ref[pl.ds(r0, BF_ROWS), lanes]
                rk = jnp.zeros((BF_ROWS, LANE), F32)
                for b in range(P_TOPK):
                    rk = rk + jnp.where(tb[b] > s2, 1.0, 0.0)
                rk_ref[pl.ds(r0, BF_ROWS), lanes] = rk.astype(BF16)
                p2h_ref[pl.ds(r0, BF_ROWS), lanes] = jnp.exp(s2 - tb[0]).astype(BF16)
                return carry

            lax.fori_loop(0, P_KEYS // BF_ROWS, rbody, 0, unroll=2)


def _route(s1, s2, s2h):
    t = s1.shape[1]
    blk = pl.BlockSpec((KROWS, RT_T), lambda i: (0, i))
    return pl.pallas_call(
        _route_kernel,
        grid=(t // RT_T,),
        in_specs=[blk, blk, blk],
        out_specs=[blk, blk, blk, blk],
        out_shape=[jax.ShapeDtypeStruct((KROWS, t), F32), jax.ShapeDtypeStruct((KROWS, t), F32),
                   jax.ShapeDtypeStruct((KROWS, t), BF16), jax.ShapeDtypeStruct((KROWS, t), BF16)],
        compiler_params=_cparams(("parallel",)),
        name="route",
    )(s1, s2, s2h)


PE_TT = 512
PE_EB = 512
PE_K1 = PE_EB // P_KEYS


PE_GL = 256


def _peer_kernel(xnt_ref, u_ref, vtp_ref, vtl_ref, cnt_ref, w_ref, rk_ref, p2h_ref, h_ref, o_ref,
                 acc, ht, gsc, cnt_s, w_s, rk_s, p2_s):
    j = pl.program_id(1)
    slot = j % 2

    @pl.when(j == 0)
    def _init():
        acc[...] = jnp.zeros(acc.shape, F32)
        ht[1] = jnp.zeros(ht.shape[1:], BF16)
        rk_s[...] = rk_ref[...]
        p2_s[...] = p2h_ref[...]

    e0 = j * PE_K1
    cnt_s[...] = cnt_ref[pl.ds(e0, PE_K1)]
    w_s[...] = w_ref[pl.ds(e0, PE_K1)]

    ngrp = P_KEYS // BF_ROWS
    bzero = jnp.zeros((), BF16)
    for s in range(PE_TT // PE_GL):
        ls = slice(s * PE_GL, (s + 1) * PE_GL)
        acc[:, ls] += jnp.dot(vtp_ref[...], ht[1 - slot, :, ls], preferred_element_type=F32)
        for c in range(PE_K1):
            gate = [jnp.zeros((BF_ROWS, PE_GL), BF16) for _ in range(ngrp)]
            for h in range(P_HEADS):
                cn = jnp.broadcast_to(cnt_s[c, h:h + 1, ls], (BF_ROWS, PE_GL)).astype(BF16)
                wg = jnp.broadcast_to(w_s[c, h:h + 1, ls], (BF_ROWS, PE_GL)).astype(BF16)
                for r in range(ngrp):
                    rows = slice(h * P_KEYS + r * BF_ROWS, h * P_KEYS + (r + 1) * BF_ROWS)
                    sel = jnp.where(rk_s[rows, ls] < cn, p2_s[rows, ls], bzero)
                    gate[r] = gate[r] + sel * wg
            for r in range(ngrp):
                rows = slice(c * P_KEYS + r * BF_ROWS, c * P_KEYS + (r + 1) * BF_ROWS)
                gsc[rows, ls] = gate[r]
        a = jnp.dot(u_ref[...], xnt_ref[:, ls], preferred_element_type=F32)
        ht[slot, :, ls] = _gelu(a).astype(BF16) * gsc[:, ls]

    @pl.when(j == pl.num_programs(1) - 1)
    def _fin():
        last = jnp.dot(vtl_ref[...], ht[slot], preferred_element_type=F32)
        o_ref[...] = h_ref[...] + (acc[...] + last).T


def _peer(xnt, u, vt, cnt, w, rk, p2h, hres):
    t = xnt.shape[1]
    nj = N_EXPERTS // PE_EB
    tok = lambda r: pl.BlockSpec((r, PE_TT), lambda i, j: (0, i))
    tok3 = pl.BlockSpec((P_KEYS, P_HEADS, PE_TT), lambda i, j: (0, 0, i))
    return pl.pallas_call(
        _peer_kernel,
        grid=(t // PE_TT, nj),
        in_specs=[tok(D_MODEL),
                  pl.BlockSpec((PE_EB, D_MODEL), lambda i, j: (j, 0)),
                  pl.BlockSpec((D_MODEL, PE_EB), lambda i, j: (0, jnp.maximum(j - 1, 0))),
                  pl.BlockSpec((D_MODEL, PE_EB), lambda i, j: (0, nj - 1)),
                  tok3, tok3, tok(KROWS), tok(KROWS),
                  pl.BlockSpec((PE_TT, D_MODEL), lambda i, j: (i, 0))],
        out_specs=pl.BlockSpec((PE_TT, D_MODEL), lambda i, j: (i, 0)),
        out_shape=jax.ShapeDtypeStruct((t, D_MODEL), F32),
        scratch_shapes=[pltpu.VMEM((D_MODEL, PE_TT), F32), pltpu.VMEM((2, PE_EB, PE_TT), BF16),
                        pltpu.VMEM((PE_EB, PE_TT), BF16),
                        pltpu.VMEM((PE_K1, P_HEADS, PE_TT), F32), pltpu.VMEM((PE_K1, P_HEADS, PE_TT), F32),
                        pltpu.VMEM((KROWS, PE_TT), BF16), pltpu.VMEM((KROWS, PE_TT), BF16)],
        compiler_params=_cparams(("parallel", "arbitrary")),
        name="peer",
    )(xnt, u, vt, vt, cnt, w, rk, p2h, hres)


def _pad_cols(w, width):
    return jnp.pad(w, ((0, 0), (0, width - w.shape[1])))


def _prep_layer(l, p):
    w_in = p["w_in"][l]
    o = 0
    cuts = {}
    for name, wdt in (("a", Q_RANK + KV_RANK + ROPE), ("bqk", GROUP_W), ("bv", GROUP_W), ("bo", GROUP_W),
                      ("bi", N_HEADS), ("bf", N_HEADS), ("c", 2 * GROUP_W), ("d", 2 * GROUP_W)):
        cuts[name] = w_in[:, o:o + wdt]
        o += wdt
    bv = cuts["bv"].reshape(D_MODEL, N_HEADS, HEAD_DIM)
    bv = jnp.pad(bv, ((0, 0), (0, 0), (0, LANE - HEAD_DIM))).reshape(D_MODEL, N_HEADS * LANE)
    w_all = jnp.concatenate([
        _pad_cols(cuts["a"], PA_W), cuts["bqk"], bv, cuts["bo"],
        _pad_cols(cuts["bi"], LANE), _pad_cols(cuts["bf"], LANE), cuts["c"], cuts["d"]], axis=1).astype(BF16)

    def head_pad(w, width):
        return jnp.pad(w, ((0, 0), (0, 0), (0, LANE - width))).reshape(w.shape[0], N_HEADS * LANE)

    wuq = head_pad(p["a_w_uq"][l].reshape(Q_RANK, N_HEADS, QK_DIM), QK_DIM).astype(BF16)
    wukv = p["a_w_ukv"][l].reshape(KV_RANK, N_HEADS, NOPE + HEAD_DIM)
    wk = head_pad(wukv[:, :, :NOPE], NOPE).astype(BF16)
    wv = head_pad(wukv[:, :, NOPE:], HEAD_DIM).astype(BF16)
    qg = _pad_cols(p["a_qn_g"][l][None, :] * (QK_DIM ** -0.5), LANE)
    kg = _pad_cols(p["a_kn_g"][l][None, :], LANE)

    eye = jnp.eye(N_HEADS, dtype=F32)
    wq_bd = jnp.einsum("hde,hg->hdge", p["b_w_q"][l], eye).reshape(GROUP_W, GROUP_W).astype(BF16)
    wk_bd = (jnp.einsum("hde,hg->hdge", p["b_w_k"][l], eye).reshape(GROUP_W, GROUP_W)
             * (HEAD_DIM ** -0.5)).astype(BF16)
    gb = jnp.concatenate([_pad_cols(p["b_b_i"][l][None, :], LANE), _pad_cols(p["b_b_f"][l][None, :], LANE)], axis=1)

    ws_cat = p["c_w_s"][l].transpose(1, 0, 2).reshape(SP_CHUNK, N_HEADS * SP_CHUNK)
    bias_full = jnp.repeat(p["c_b_s"][l].T, HEAD_DIM, axis=1)

    wpq = p["p_w_q"][l].reshape(D_MODEL, P_HEADS, 2, P_HALF).transpose(0, 2, 1, 3).reshape(D_MODEL, 2 * PQ_W)
    keys = p["p_sub_keys"][l]
    eye8 = jnp.eye(P_HEADS, dtype=F32)
    kmat = [jnp.einsum("hnd,hg->nhgd", keys[:, s], eye8).reshape(KROWS, PQ_W).astype(BF16) for s in range(2)]
    return dict(
        norm1_g=p["norm1_g"][l][None, :], w_all=w_all,
        cqg=p["a_cq_g"][l][None, :], ckvg=p["a_ckv_g"][l][None, :], wuq=wuq, wk=wk, wv=wv, qg=qg, kg=kg,
        b_cw=p["b_conv_w"][l], b_cb=p["b_conv_b"][l][None, :], wq_bd=wq_bd, wk_bd=wk_bd, gb=gb,
        hng=p["b_hn_g"][l].reshape(1, GROUP_W),
        c_g=p["c_ln_g"][l][None, :], c_b=p["c_ln_b"][l][None, :], ws_cat=ws_cat, bias_full=bias_full,
        d_w=p["d_dw_w"][l], d_b=p["d_dw_b"][l][None, :],
        d_g=p["d_cn_g"][l].reshape(1, GROUP_W), d_be=p["d_cn_b"][l].reshape(1, GROUP_W),
        w_out=p["w_out"][l].astype(BF16), norm2_g=p["norm2_g"][l][None, :], wpq=wpq.astype(BF16),
        k1=kmat[0], k2=kmat[1], k2h=keys[:, 1].astype(BF16),
        u=p["p_u"][l].astype(BF16), vt=p["p_v"][l].T.astype(BF16),
    )


def _layer(x, rope, consts, w, bsz, seq):
    t = bsz * seq
    pa, pb, pg, pc, pd = _inproj(x, w["norm1_g"], w["w_all"])
    r3 = lambda a: a.reshape(bsz, seq, a.shape[-1])
    mix_a = _mla(r3(pa), r3(rope[0]), r3(rope[1]), w["cqg"], w["ckvg"], w["wuq"], w["wk"], w["wv"],
                 w["qg"], w["kg"], bsz, seq).reshape(t, GROUP_W)
    mix_b = _mlstm(r3(pb), r3(pg), w["b_cw"], w["b_cb"], w["wq_bd"], w["wk_bd"], w["gb"], w["hng"],
                   consts["tri"], bsz, seq).reshape(t, GROUP_W)
    mix_c = _spatial(pc, w["c_g"], w["c_b"], w["ws_cat"], w["bias_full"])
    mix_d = _convmod(r3(pd), w["d_w"], w["d_b"], consts["avg"], w["d_g"], w["d_be"], bsz, seq).reshape(t, GROUP_W)
    h, xnt, s1, s2, s2h = _outproj(x, mix_a, mix_b, mix_c, mix_d, w["w_out"], w["norm2_g"], w["wpq"],
                                   w["k1"], w["k2"], w["k2h"])
    cnt, wgt, rk, p2h = _route(s1, s2, s2h)
    shp = (P_KEYS, P_HEADS, t)
    return _peer(xnt, w["u"], w["vt"], cnt.reshape(shp), wgt.reshape(shp), rk, p2h, h)


def _consts():
    half = ROPE // 2
    inv_freq = ROPE_BASE ** (-jnp.arange(0, ROPE, 2, dtype=F32) / ROPE)
    invf = jnp.zeros((1, LANE), F32)
    invf = invf.at[0, NOPE:NOPE + half].set(inv_freq).at[0, NOPE + half:QK_DIM].set(inv_freq)
    tri = jnp.tril(jnp.ones((ML_L, ML_L), F32)).astype(BF16)
    grp = jnp.arange(GROUP_W) // HEAD_DIM
    avg = ((grp[:, None] == grp[None, :]).astype(F32) / HEAD_DIM).astype(BF16)
    return dict(invf=invf, tri=tri, avg=avg)


def kernel(x, positions, norm1_g, w_in, a_cq_g, a_ckv_g, a_w_uq, a_w_ukv, a_qn_g, a_kn_g, b_conv_w, b_conv_b, b_w_q, b_w_k, b_b_i, b_b_f, b_hn_g, c_ln_g, c_ln_b, c_w_s, c_b_s, d_dw_w, d_dw_b, d_cn_g, d_cn_b, w_out, norm2_g, p_w_q, p_sub_keys, p_u, p_v):
    params = dict(norm1_g=norm1_g, w_in=w_in, a_cq_g=a_cq_g, a_ckv_g=a_ckv_g, a_w_uq=a_w_uq, a_w_ukv=a_w_ukv,
                  a_qn_g=a_qn_g, a_kn_g=a_kn_g, b_conv_w=b_conv_w, b_conv_b=b_conv_b, b_w_q=b_w_q, b_w_k=b_w_k,
                  b_b_i=b_b_i, b_b_f=b_b_f, b_hn_g=b_hn_g, c_ln_g=c_ln_g, c_ln_b=c_ln_b, c_w_s=c_w_s,
                  c_b_s=c_b_s, d_dw_w=d_dw_w, d_dw_b=d_dw_b, d_cn_g=d_cn_g, d_cn_b=d_cn_b, w_out=w_out,
                  norm2_g=norm2_g, p_w_q=p_w_q, p_sub_keys=p_sub_keys, p_u=p_u, p_v=p_v)
    bsz, seq, d = x.shape
    consts = _consts()
    rope = _rope_tables(positions.reshape(bsz * seq, 1), consts["invf"])
    xf = x.reshape(bsz * seq, d)
    for l in range(norm1_g.shape[0]):
        xf = _layer(xf, rope, consts, _prep_layer(l, params), bsz, seq)
    return xf.reshape(bsz, seq, d)
```

```python
import functools

import jax
import jax.numpy as jnp
from jax import lax
from jax.experimental import pallas as pl
from jax.experimental.pallas import tpu as pltpu

F32 = jnp.float32
BF16 = jnp.bfloat16

D_MODEL = 1024
N_HEADS = 4
HEAD_DIM = 64
GROUP_W = N_HEADS * HEAD_DIM
Q_RANK, KV_RANK = 256, 128
NOPE, ROPE = 64, 32
QK_DIM = NOPE + ROPE
ROPE_BASE = 10000.0
LANE = 128
SUBLANES = 8
CONV_B = 4
CONV_D = 31
SP_CHUNK = 128
P_HEADS = 8
P_KEYS = 128
P_TOPK = 16
P_HALF = 64
N_EXPERTS = P_KEYS * P_KEYS
EPS = 1e-6
NEG = -1e30
VMEM_LIMIT = 56 * 1024 * 1024

NT = (((1,), (1,)), ((), ()))


def _cparams(sem, flags=None):
    return pltpu.CompilerParams(dimension_semantics=sem, vmem_limit_bytes=VMEM_LIMIT, flags=flags)


def _full(shape):
    n = len(shape)
    return pl.BlockSpec(shape, lambda *_: (0,) * n)


def _split3(a):
    hi = a.astype(BF16)
    r1 = a - hi.astype(F32)
    mid = r1.astype(BF16)
    lo = (r1 - mid.astype(F32)).astype(BF16)
    return hi, mid, lo


def _dot_exact_rhs(a, m_bf16):
    hi, mid, lo = _split3(a)
    d = lambda t: jnp.dot(t, m_bf16, preferred_element_type=F32)
    return d(hi) + d(mid) + d(lo)


def _dot_exact_lhs(m_bf16, a):
    hi, mid, lo = _split3(a)
    d = lambda t: jnp.dot(m_bf16, t, preferred_element_type=F32)
    return d(hi) + d(mid) + d(lo)


def _sigmoid(x):
    return 1.0 / (1.0 + jnp.exp(-x))


def _gelu(x):
    c = lambda v: jnp.asarray(v, x.dtype)
    inner = (c(0.7978845608028654) * x) * (c(1.0) + c(0.044715) * (x * x))
    return (c(0.5) * x) * (c(1.0) + jnp.tanh(inner))


PA_W, PB_W, PG_W, PC_W, PD_W = 512, 1024, 256, 512, 512
IN_COLS = (0, PA_W, PA_W + PB_W, PA_W + PB_W + PG_W, PA_W + PB_W + PG_W + PC_W,
           PA_W + PB_W + PG_W + PC_W + PD_W)


def _inproj_kernel(x_ref, g_ref, w_ref, pa_ref, pb_ref, pg_ref, pc_ref, pd_ref):
    x = x_ref[...]
    hn = (x * lax.rsqrt(jnp.mean(x * x, axis=-1, keepdims=True) + EPS) * g_ref[...]).astype(BF16)
    outs = (pa_ref, pb_ref, pg_ref, pc_ref, pd_ref)
    for k, o in enumerate(outs):
        y = jnp.dot(hn, w_ref[:, IN_COLS[k]:IN_COLS[k + 1]], preferred_element_type=F32)
        o[...] = y.astype(o.dtype)


def _inproj(x, g, w_all, tm=512):
    t = x.shape[0]
    widths = (PA_W, PB_W, PG_W, PC_W, PD_W)
    dts = (BF16, BF16, F32, BF16, BF16)
    return pl.pallas_call(
        _inproj_kernel,
        grid=(t // tm,),
        in_specs=[pl.BlockSpec((tm, D_MODEL), lambda i: (i, 0)),
                  _full((1, D_MODEL)), _full((D_MODEL, IN_COLS[-1]))],
        out_specs=[pl.BlockSpec((tm, w), lambda i: (i, 0)) for w in widths],
        out_shape=[jax.ShapeDtypeStruct((t, w), d) for w, d in zip(widths, dts)],
        compiler_params=_cparams(("parallel",)),
        name="inproj",
    )(x, g, w_all)


MLA_TQ = 256
MLA_PRO = 512


def _rope_kernel(pos_ref, invf_ref, cos_ref, sin_ref):
    ang = pos_ref[...].astype(F32) * invf_ref[...]
    cos_ref[...] = jnp.cos(ang)
    sin_ref[...] = jnp.sin(ang)


def _rope_tables(pos, invf, tm=1024):
    t = pos.shape[0]
    blk = pl.BlockSpec((tm, LANE), lambda i: (i, 0))
    return pl.pallas_call(
        _rope_kernel,
        grid=(t // tm,),
        in_specs=[pl.BlockSpec((tm, 1), lambda i: (i, 0)), _full((1, LANE))],
        out_specs=[blk, blk],
        out_shape=[jax.ShapeDtypeStruct((t, LANE), F32)] * 2,
        compiler_params=_cparams(("parallel",)),
        name="rope",
    )(pos, invf)


def _mla_kernel(pa_ref, cos_ref, sin_ref, cqg_ref, ckvg_ref, wuq_ref, wk_ref, wv_ref,
                qg_ref, kg_ref, o_ref, q_s, k_s, v_s, *, seq):
    tq = MLA_TQ

    def chunk(c, carry):
        r0 = pl.multiple_of(c * MLA_PRO, MLA_PRO)
        pa = pa_ref[0, pl.ds(r0, MLA_PRO), :]
        cq = pa[:, 0:Q_RANK].astype(F32)
        ckv = pa[:, Q_RANK:Q_RANK + KV_RANK].astype(F32)
        krp = pa[:, Q_RANK + KV_RANK:].astype(F32)
        cqn = (cq * lax.rsqrt(jnp.mean(cq * cq, -1, keepdims=True) + EPS) * cqg_ref[...]).astype(BF16)
        ckvn = (ckv * lax.rsqrt(jnp.mean(ckv * ckv, -1, keepdims=True) + EPS) * ckvg_ref[...]).astype(BF16)
        q = jnp.dot(cqn, wuq_ref[...], preferred_element_type=F32)
        kn = jnp.dot(ckvn, wk_ref[...], preferred_element_type=F32)
        v = jnp.dot(ckvn, wv_ref[...], preferred_element_type=F32)
        lane = lax.broadcasted_iota(jnp.int32, (MLA_PRO, LANE), 1)
        cosv = cos_ref[0, pl.ds(r0, MLA_PRO), :]
        sinv = sin_ref[0, pl.ds(r0, MLA_PRO), :]
        half = ROPE // 2
        sin_hi = jnp.where((lane >= NOPE + half) & (lane < QK_DIM), sinv, 0.0)
        sin_lo = jnp.where((lane >= NOPE) & (lane < NOPE + half), -sinv, 0.0)
        kr_sh = pltpu.roll(krp, NOPE, 1)

        def norm_rope(t, g):
            t = t * lax.rsqrt(jnp.sum(t * t, -1, keepdims=True) * (1.0 / QK_DIM) + EPS) * g
            return (t * cosv + pltpu.roll(t, half, 1) * sin_hi
                    + pltpu.roll(t, LANE - half, 1) * sin_lo)

        for h in range(N_HEADS):
            sl = slice(h * LANE, (h + 1) * LANE)
            q_s[pl.ds(r0, MLA_PRO), sl] = norm_rope(q[:, sl], qg_ref[...]).astype(BF16)
            k_s[pl.ds(r0, MLA_PRO), sl] = norm_rope(kn[:, sl] + kr_sh, kg_ref[...]).astype(BF16)
            v_s[pl.ds(r0, MLA_PRO), sl] = jnp.where(lane == HEAD_DIM, 1.0, v[:, sl]).astype(BF16)
        return carry

    lax.fori_loop(0, seq // MLA_PRO, chunk, 0)

    row = lax.broadcasted_iota(jnp.int32, (tq, tq), 0)
    col = lax.broadcasted_iota(jnp.int32, (tq, tq), 1)
    for i in range(seq // tq):
        klen = (i + 1) * tq
        for h in range(N_HEADS):
            sl = slice(h * LANE, (h + 1) * LANE)
            qh = q_s[i * tq:(i + 1) * tq, sl]
            s = lax.dot_general(qh, k_s[0:klen, sl], NT, preferred_element_type=F32)
            diag = jnp.where(col <= row, s[:, klen - tq:], NEG)
            s = diag if i == 0 else jnp.concatenate([s[:, :klen - tq], diag], axis=1)
            p = jnp.exp(s - jnp.max(s, -1, keepdims=True)).astype(BF16)
            acc = jnp.dot(p, v_s[0:klen, sl], preferred_element_type=F32)
            out_h = acc[:, :HEAD_DIM] / acc[:, HEAD_DIM:HEAD_DIM + 1]
            o_ref[0, i * tq:(i + 1) * tq, h * HEAD_DIM:(h + 1) * HEAD_DIM] = out_h.astype(BF16)


def _mla(pa, cosv, sinv, cqg, ckvg, wuq, wk, wv, qg, kg, bsz, seq):
    hw = N_HEADS * LANE
    seq_blk = lambda w: pl.BlockSpec((1, seq, w), lambda b: (b, 0, 0))
    return pl.pallas_call(
        functools.partial(_mla_kernel, seq=seq),
        grid=(bsz,),
        in_specs=[seq_blk(PA_W), seq_blk(LANE), seq_blk(LANE),
                  _full((1, Q_RANK)), _full((1, KV_RANK)),
                  _full((Q_RANK, hw)), _full((KV_RANK, hw)), _full((KV_RANK, hw)),
                  _full((1, LANE)), _full((1, LANE))],
        out_specs=seq_blk(GROUP_W),
        out_shape=jax.ShapeDtypeStruct((bsz, seq, GROUP_W), BF16),
        scratch_shapes=[pltpu.VMEM((seq, hw), BF16)] * 3,
        compiler_params=_cparams(("parallel",)),
        name="mla",
    )(pa, cosv, sinv, cqg, ckvg, wuq, wk, wv, qg, kg)


ML_L = 256
ML_HALO = 8


def _mlstm_kernel(pb_ref, pg_ref, cw_ref, cb_ref, wq_ref, wk_ref, gb_ref, hng_ref, tri_ref,
                  o_ref, xpad, st, mst, fst):
    c = pl.program_id(1)
    L = ML_L

    @pl.when(c == 0)
    def _init():
        xpad[0:ML_HALO, :] = jnp.zeros((ML_HALO, GROUP_W), F32)
        st[...] = jnp.zeros(st.shape, F32)
        mst[...] = jnp.full(mst.shape, NEG, F32)
        fst[...] = jnp.zeros(fst.shape, F32)

    xqk = pb_ref[0, :, 0:GROUP_W].astype(F32)
    xpad[ML_HALO:ML_HALO + L, :] = xqk
    acc = cb_ref[...] + jnp.zeros((L, GROUP_W), F32)
    for j in range(CONV_B):
        off = ML_HALO - (CONV_B - 1) + j
        acc = acc + cw_ref[j:j + 1, :] * xpad[off:off + L, :]
    xpad[0:ML_HALO, :] = xqk[L - ML_HALO:, :]
    xc = (acc * _sigmoid(acc)).astype(BF16)
    q = jnp.dot(xc, wq_ref[...], preferred_element_type=F32)
    k = jnp.dot(xc, wk_ref[...], preferred_element_type=F32)

    gates = pg_ref[0] + gb_ref[...]
    ig = gates[:, :LANE]
    fg = gates[:, LANE:]
    lf = jnp.minimum(fg, 0.0) - jnp.log(1.0 + jnp.exp(-jnp.abs(fg)))
    fcol = _dot_exact_lhs(tri_ref[...], lf) + fst[...]
    fst[...] = fcol[L - 1:L, :]
    a_col = ig - fcol
    a_t = a_col.T

    rowi = lax.broadcasted_iota(jnp.int32, (L, L), 0)
    coli = lax.broadcasted_iota(jnp.int32, (L, L), 1)
    causal = coli <= rowi
    lane_v = lax.broadcasted_iota(jnp.int32, (L, LANE), 1)
    lane_w = lax.broadcasted_iota(jnp.int32, (L, GROUP_W), 1)

    mts, mns, mcs = [], [], []
    for h in range(N_HEADS):
        amat = jnp.where(causal, a_t[h:h + 1, :], NEG)
        mc = mst[:, h:h + 1]
        mt = jnp.maximum(jnp.max(amat, -1, keepdims=True), mc)
        mts.append((amat, mt))
        mcs.append(mc)
        mns.append(mt[L - 1:L, :])

    wg = jnp.exp(a_col[:, N_HEADS - 1:N_HEADS] - mns[N_HEADS - 1])
    for h in range(N_HEADS - 2, -1, -1):
        wg = jnp.where(lane_w < (h + 1) * HEAD_DIM, jnp.exp(a_col[:, h:h + 1] - mns[h]), wg)
    kw_t = (k * wg).T.astype(BF16)

    for h in range(N_HEADS):
        hs = slice(h * HEAD_DIM, (h + 1) * HEAD_DIM)
        amat, mt = mts[h]
        mc, mn = mcs[h], mns[h]
        p = jnp.exp(amat - mt)
        qh = q[:, hs].astype(BF16)
        kh = k[:, hs].astype(BF16)
        qk = lax.dot_general(qh, kh, NT, preferred_element_type=F32)
        w = (p * qk).astype(BF16)
        vext = jnp.where(lane_v == HEAD_DIM, 1.0,
                         pb_ref[0, :, GROUP_W + h * LANE:GROUP_W + (h + 1) * LANE].astype(F32)).astype(BF16)
        sth = st[h]
        nd = (jnp.dot(w, vext, preferred_element_type=F32)
              + jnp.exp(mc - mt) * jnp.dot(qh, sth.astype(BF16), preferred_element_type=F32))
        den = nd[:, HEAD_DIM:HEAD_DIM + 1]
        floor = jnp.exp(-(fcol[:, h:h + 1] + mt))
        hh = nd[:, :HEAD_DIM] / jnp.maximum(jnp.abs(den), floor)
        hh = hh * lax.rsqrt(jnp.mean(hh * hh, -1, keepdims=True) + EPS) * hng_ref[:, hs]
        xo = pb_ref[0, :, GROUP_W + N_HEADS * LANE + h * HEAD_DIM:
                    GROUP_W + N_HEADS * LANE + (h + 1) * HEAD_DIM].astype(F32)
        o_ref[0, :, hs] = (_sigmoid(xo) * hh).astype(BF16)
        st[h] = jnp.exp(mc - mn) * sth + jnp.dot(kw_t[hs, :], vext, preferred_element_type=F32)
        mst[:, h:h + 1] = mn


def _mlstm(pb, pg, cw, cb, wq, wk, gb, hng, tri, bsz, seq):
    L = ML_L
    return pl.pallas_call(
        _mlstm_kernel,
        grid=(bsz, seq // L),
        in_specs=[pl.BlockSpec((1, L, PB_W), lambda b, c: (b, c, 0)),
                  pl.BlockSpec((1, L, PG_W), lambda b, c: (b, c, 0)),
                  _full((CONV_B, GROUP_W)), _full((1, GROUP_W)),
                  _full((GROUP_W, GROUP_W)), _full((GROUP_W, GROUP_W)),
                  _full((1, PG_W)), _full((1, GROUP_W)), _full((L, L))],
        out_specs=pl.BlockSpec((1, L, GROUP_W), lambda b, c: (b, c, 0)),
        out_shape=jax.ShapeDtypeStruct((bsz, seq, GROUP_W), BF16),
        scratch_shapes=[pltpu.VMEM((L + ML_HALO, GROUP_W), F32),
                        pltpu.VMEM((N_HEADS, HEAD_DIM, LANE), F32),
                        pltpu.VMEM((1, LANE), F32),
                        pltpu.VMEM((1, LANE), F32)],
        compiler_params=_cparams(("parallel", "arbitrary")),
        name="mlstm",
    )(pb, pg, cw, cb, wq, wk, gb, hng, tri)


SP_STEP = 4


def _spatial_kernel(pc_ref, g_ref, b_ref, ws_ref, bias_ref, o_ref):
    T = SP_CHUNK
    rowi = lax.broadcasted_iota(jnp.int32, (T, N_HEADS * T), 0)
    coli = lax.broadcasted_iota(jnp.int32, (T, N_HEADS * T), 1)
    ws = jnp.where((coli & (T - 1)) <= rowi, ws_ref[...], 0.0).astype(BF16)
    rgrp = lax.broadcasted_iota(jnp.int32, (T, GROUP_W), 1) // HEAD_DIM
    for cidx in range(SP_STEP):
        rs = slice(cidx * T, (cidx + 1) * T)
        u = _gelu(pc_ref[rs, 0:GROUP_W].astype(F32))
        gv = _gelu(pc_ref[rs, GROUP_W:].astype(F32))
        mu = jnp.mean(gv, -1, keepdims=True)
        dv = gv - mu
        var = jnp.mean(dv * dv, -1, keepdims=True)
        vn = (dv * lax.rsqrt(var + EPS) * g_ref[...] + b_ref[...]).astype(BF16)
        zero = jnp.zeros_like(vn)
        vbig = jnp.concatenate([jnp.where(rgrp == g, vn, zero) for g in range(N_HEADS)], axis=0)
        sg = jnp.dot(ws, vbig, preferred_element_type=F32) + bias_ref[...]
        o_ref[rs, :] = (u * sg).astype(BF16)


def _spatial(pc, g, b, ws_cat, bias_full):
    t = pc.shape[0]
    tm = SP_STEP * SP_CHUNK
    return pl.pallas_call(
        _spatial_kernel,
        grid=(t // tm,),
        in_specs=[pl.BlockSpec((tm, PC_W), lambda i: (i, 0)),
                  _full((1, GROUP_W)), _full((1, GROUP_W)),
                  _full((SP_CHUNK, N_HEADS * SP_CHUNK)), _full((SP_CHUNK, GROUP_W))],
        out_specs=pl.BlockSpec((tm, GROUP_W), lambda i: (i, 0)),
        out_shape=jax.ShapeDtypeStruct((t, GROUP_W), BF16),
        compiler_params=_cparams(("parallel",)),
        name="spatial",
    )(pc, g, b, ws_cat, bias_full)


CV_T = 256
CV_HALO = 32


def _convmod_kernel(pd_ref, w_ref, b_ref, avg_ref, g_ref, be_ref, o_ref, ypad, zsh):
    i = pl.program_id(1)
    T = CV_T

    @pl.when(i == 0)
    def _init():
        ypad[0:CV_HALO, :] = jnp.zeros((CV_HALO, GROUP_W), F32)

    a = pd_ref[0, :, 0:GROUP_W].astype(F32)
    b = pd_ref[0, :, GROUP_W:].astype(F32)
    y = a * _sigmoid(b)
    ypad[CV_HALO:CV_HALO + T, :] = y
    acc = b_ref[...] + jnp.zeros((T, GROUP_W), F32)
    off0 = CV_HALO - (CONV_D - 1)
    for b in range(SUBLANES):
        offs = [o for o in range(off0, off0 + CONV_D) if o % SUBLANES == b]
        span = offs[-1] - b + T
        src = ypad
        if b:
            zsh[0:span, :] = ypad[b:b + span, :]
            src = zsh
        for o in offs:
            acc = acc + w_ref[o - off0:o - off0 + 1, :] * src[o - b:o - b + T, :]
    ypad[0:CV_HALO, :] = y[T - CV_HALO:, :]
    mu = _dot_exact_rhs(acc, avg_ref[...])
    dv = acc - mu
    var = _dot_exact_rhs(dv * dv, avg_ref[...])
    yn = dv * lax.rsqrt(var + EPS) * g_ref[...] + be_ref[...]
    o_ref[0] = (yn * _sigmoid(yn)).astype(BF16)


def _convmod(pd, w, b, avg, g, be, bsz, seq):
    return pl.pallas_call(
        _convmod_kernel,
        grid=(bsz, seq // CV_T),
        in_specs=[pl.BlockSpec((1, CV_T, PD_W), lambda bb, i: (bb, i, 0)),
                  _full((CONV_D, GROUP_W)), _full((1, GROUP_W)), _full((GROUP_W, GROUP_W)),
                  _full((1, GROUP_W)), _full((1, GROUP_W))],
        out_specs=pl.BlockSpec((1, CV_T, GROUP_W), lambda bb, i: (bb, i, 0)),
        out_shape=jax.ShapeDtypeStruct((bsz, seq, GROUP_W), BF16),
        scratch_shapes=[pltpu.VMEM((CV_T + CV_HALO, GROUP_W), F32)] * 2,
        compiler_params=_cparams(("parallel", "arbitrary")),
        name="convmod",
    )(pd, w, b, avg, g, be)


OP_TM = 512
PQ_W = P_HEADS * P_HALF
KROWS = P_KEYS * P_HEADS


def _outproj_kernel(x_ref, ma_ref, mb_ref, mc_ref, md_ref, wo_ref, g_ref, wpq_ref, k1_ref, k2_ref,
                    k2h_ref, h_ref, xnt_ref, s1_ref, s2_ref, s2h_ref):
    acc = x_ref[...]
    for g, m in enumerate((ma_ref, mb_ref, mc_ref, md_ref)):
        acc = acc + jnp.dot(m[...], wo_ref[g * GROUP_W:(g + 1) * GROUP_W, :], preferred_element_type=F32)
    h_ref[...] = acc
    hn = acc * lax.rsqrt(jnp.mean(acc * acc, -1, keepdims=True) + EPS) * g_ref[...]
    xnt_ref[...] = hn.T.astype(BF16)
    q = jnp.dot(hn.astype(BF16), wpq_ref[...], preferred_element_type=F32).astype(BF16)
    s1_ref[...] = lax.dot_general(k1_ref[...], q[:, :PQ_W], NT, preferred_element_type=F32)
    s2_ref[...] = lax.dot_general(k2_ref[...], q[:, PQ_W:], NT, preferred_element_type=F32)
    for h in range(P_HEADS):
        qh = q[:, PQ_W + h * P_HALF:PQ_W + (h + 1) * P_HALF]
        s2h_ref[h * P_KEYS:(h + 1) * P_KEYS, :] = lax.dot_general(
            k2h_ref[h], qh, NT, preferred_element_type=F32)


def _outproj(x, ma, mb, mc, md, wo, g, wpq, k1, k2, k2h):
    t = x.shape[0]
    tm = OP_TM
    mix = pl.BlockSpec((tm, GROUP_W), lambda i: (i, 0))
    tcol = pl.BlockSpec((KROWS, tm), lambda i: (0, i))
    return pl.pallas_call(
        _outproj_kernel,
        grid=(t // tm,),
        in_specs=[pl.BlockSpec((tm, D_MODEL), lambda i: (i, 0)), mix, mix, mix, mix,
                  _full((D_MODEL, D_MODEL)), _full((1, D_MODEL)), _full((D_MODEL, 2 * PQ_W)),
                  _full((KROWS, PQ_W)), _full((KROWS, PQ_W)), _full((P_HEADS, P_KEYS, P_HALF))],
        out_specs=[pl.BlockSpec((tm, D_MODEL), lambda i: (i, 0)),
                   pl.BlockSpec((D_MODEL, tm), lambda i: (0, i)), tcol, tcol, tcol],
        out_shape=[jax.ShapeDtypeStruct((t, D_MODEL), F32),
                   jax.ShapeDtypeStruct((D_MODEL, t), BF16),
                   jax.ShapeDtypeStruct((KROWS, t), F32),
                   jax.ShapeDtypeStruct((KROWS, t), F32),
                   jax.ShapeDtypeStruct((KROWS, t), F32)],
        compiler_params=_cparams(("parallel",)),
        name="outproj",
    )(x, ma, mb, mc, md, wo, g, wpq, k1, k2, k2h)


RT_T = 256
BF_ROWS = 16


def _bitonic_merge(v):
    n = len(v)
    if n == 1:
        return v
    half = n // 2
    hi = [jnp.maximum(v[i], v[i + half]) for i in range(half)]
    lo = [jnp.minimum(v[i], v[i + half]) for i in range(half)]
    return _bitonic_merge(hi) + _bitonic_merge(lo)


def _bitonic_sort(v):
    n = len(v)
    if n == 1:
        return v
    return _bitonic_merge(_bitonic_sort(v[:n // 2]) + _bitonic_sort(v[n // 2:])[::-1])


def _merge_top(a, b):
    n = len(a)
    c = list(a)
    for k, bv in enumerate(b):
        c[n - 1 - k] = jnp.maximum(a[n - 1 - k], bv)
    return _bitonic_merge(c)


def _sorted_top(ref, lanes):
    top = None
    for grp in range(P_KEYS // P_TOPK):
        tiles = [ref[(grp * P_TOPK + k) * P_HEADS:(grp * P_TOPK + k + 1) * P_HEADS, lanes]
                 for k in range(P_TOPK)]
        srt = _bitonic_sort(tiles)
        top = srt if top is None else _merge_top(top, srt)
    return top


def _route_kernel(s1_ref, s2_ref, s2h_ref, cnt_ref, w_ref, rk_ref, p2h_ref):
    for g in range(RT_T // LANE):
        lanes = slice(g * LANE, (g + 1) * LANE)
        t1 = _sorted_top(s1_ref, lanes)
        t2 = _sorted_top(s2_ref, lanes)
        rows = [[t1[a] + t2[b] for b in range(P_TOPK // (a + 1))] for a in range(P_TOPK)]
        top = rows[0]
        for a in range(1, P_TOPK):
            top = _merge_top(top, rows[a])
        tau = top[P_TOPK - 1]
        cmax = rows[0][0]
        z = jnp.zeros((P_HEADS, LANE), F32)
        cnt_a = []
        for a in range(P_TOPK):
            ca = jnp.zeros((P_HEADS, LANE), F32)
            for cnd in rows[a]:
                sel = cnd >= tau
                z = z + jnp.where(sel, jnp.exp(cnd - cmax), 0.0)
                ca = ca + jnp.where(sel, 1.0, 0.0)
            cnt_a.append(ca)
        zinv = 1.0 / z

        def body(n, carry):
            r0 = pl.multiple_of(n * P_HEADS, P_HEADS)
            s1 = s1_ref[pl.ds(r0, P_HEADS), lanes]
            cn = jnp.zeros((P_HEADS, LANE), F32)
            for a in range(P_TOPK):
                cn = jnp.where(s1 == t1[a], cnt_a[a], cn)
            cnt_ref[pl.ds(r0, P_HEADS), lanes] = cn
            w_ref[pl.ds(r0, P_HEADS), lanes] = jnp.exp(s1 - t1[0]) * zinv
            return carry

        lax.fori_loop(0, P_KEYS, body, 0, unroll=4)

        for h in range(P_HEADS):
            tb = [jnp.broadcast_to(t2[b][h:h + 1, :], (BF_ROWS, LANE)) for b in range(P_TOPK)]

            def rbody(r, carry, h=h, tb=tb):
                r0 = pl.multiple_of(h * P_KEYS + r * BF_ROWS, BF_ROWS)
                s2 = s2h_ref[pl.ds(r0, BF_ROWS), lanes]
                rk = jnp.full((BF_ROWS, LANE), float(P_TOPK), F32)
                for b in range(P_TOPK - 1, -1, -1):
                    rk = jnp.where(tb[b] > s2, rk, float(b))
                rk_ref[pl.ds(r0, BF_ROWS), lanes] = rk.astype(BF16)
                p2h_ref[pl.ds(r0, BF_ROWS), lanes] = jnp.exp(s2 - tb[0]).astype(BF16)
                return carry

            lax.fori_loop(0, P_KEYS // BF_ROWS, rbody, 0, unroll=2)


def _route(s1, s2, s2h):
    t = s1.shape[1]
    blk = pl.BlockSpec((KROWS, RT_T), lambda i: (0, i))
    return pl.pallas_call(
        _route_kernel,
        grid=(t // RT_T,),
        in_specs=[blk, blk, blk],
        out_specs=[blk, blk, blk, blk],
        out_shape=[jax.ShapeDtypeStruct((KROWS, t), F32), jax.ShapeDtypeStruct((KROWS, t), F32),
                   jax.ShapeDtypeStruct((KROWS, t), BF16), jax.ShapeDtypeStruct((KROWS, t), BF16)],
        compiler_params=_cparams(("parallel",)),
        name="route",
    )(s1, s2, s2h)


PE_TT = 512
PE_EB = 1024
PE_K1 = PE_EB // P_KEYS


PE_GL = 256


def _peer_kernel(xnt_ref, u_ref, vtp_ref, vtl_ref, cnt_ref, w_ref, rk_ref, p2h_ref, h_ref, o_ref,
                 acc, ht, gsc, cnt_s, w_s, rk_s, p2_s):
    j = pl.program_id(1)
    slot = j % 2

    @pl.when(j == 0)
    def _init():
        acc[...] = jnp.zeros(acc.shape, F32)
        ht[1] = jnp.zeros(ht.shape[1:], BF16)
        rk_s[...] = rk_ref[...]
        p2_s[...] = p2h_ref[...]

    e0 = j * PE_K1
    cnt_s[...] = cnt_ref[pl.ds(e0, PE_K1)]
    w_s[...] = w_ref[pl.ds(e0, PE_K1)]

    ngrp = P_KEYS // BF_ROWS
    bzero = jnp.zeros((), BF16)
    for s in range(PE_TT // PE_GL):
        ls = slice(s * PE_GL, (s + 1) * PE_GL)
        acc[:, ls] += jnp.dot(vtp_ref[...], ht[1 - slot, :, ls], preferred_element_type=F32)
        for c in range(PE_K1):
            gate = [jnp.zeros((BF_ROWS, PE_GL), BF16) for _ in range(ngrp)]
            for h in range(P_HEADS):
                cn = jnp.broadcast_to(cnt_s[c, h:h + 1, ls], (BF_ROWS, PE_GL)).astype(BF16)
                wg = jnp.broadcast_to(w_s[c, h:h + 1, ls], (BF_ROWS, PE_GL)).astype(BF16)
                for r in range(ngrp):
                    rows = slice(h * P_KEYS + r * BF_ROWS, h * P_KEYS + (r + 1) * BF_ROWS)
                    sel = jnp.where(rk_s[rows, ls] < cn, p2_s[rows, ls], bzero)
                    gate[r] = gate[r] + sel * wg
            for r in range(ngrp):
                rows = slice(c * P_KEYS + r * BF_ROWS, c * P_KEYS + (r + 1) * BF_ROWS)
                gsc[rows, ls] = gate[r]
        a = jnp.dot(u_ref[...], xnt_ref[:, ls], preferred_element_type=F32)
        ht[slot, :, ls] = _gelu(a.astype(BF16)) * gsc[:, ls]

    @pl.when(j == pl.num_programs(1) - 1)
    def _fin():
        last = jnp.dot(vtl_ref[...], ht[slot], preferred_element_type=F32)
        o_ref[...] = h_ref[...] + (acc[...] + last).T


def _peer(xnt, u_all, vt_all, layer, cnt, w, rk, p2h, hres):
    t = xnt.shape[1]
    nj = N_EXPERTS // PE_EB
    tok = lambda r: pl.BlockSpec((r, PE_TT), lambda i, j: (0, i))
    tok3 = pl.BlockSpec((P_KEYS, P_HEADS, PE_TT), lambda i, j: (0, 0, i))
    return pl.pallas_call(
        _peer_kernel,
        grid=(t // PE_TT, nj),
        in_specs=[tok(D_MODEL),
                  pl.BlockSpec((None, PE_EB, D_MODEL), lambda i, j: (layer, j, 0)),
                  pl.BlockSpec((None, D_MODEL, PE_EB), lambda i, j: (layer, 0, jnp.maximum(j - 1, 0))),
                  pl.BlockSpec((None, D_MODEL, PE_EB), lambda i, j: (layer, 0, nj - 1)),
                  tok3, tok3, tok(KROWS), tok(KROWS),
                  pl.BlockSpec((PE_TT, D_MODEL), lambda i, j: (i, 0))],
        out_specs=pl.BlockSpec((PE_TT, D_MODEL), lambda i, j: (i, 0)),
        out_shape=jax.ShapeDtypeStruct((t, D_MODEL), F32),
        scratch_shapes=[pltpu.VMEM((D_MODEL, PE_TT), F32), pltpu.VMEM((2, PE_EB, PE_TT), BF16),
                        pltpu.VMEM((PE_EB, PE_TT), BF16),
                        pltpu.VMEM((PE_K1, P_HEADS, PE_TT), F32), pltpu.VMEM((PE_K1, P_HEADS, PE_TT), F32),
                        pltpu.VMEM((KROWS, PE_TT), BF16), pltpu.VMEM((KROWS, PE_TT), BF16)],
        compiler_params=_cparams(("parallel", "arbitrary")),
        name="peer",
    )(xnt, u_all, vt_all, vt_all, cnt, w, rk, p2h, hres)


def _pad_cols(w, width):
    return jnp.pad(w, ((0, 0), (0, width - w.shape[1])))


def _prep_layer(l, p):
    w_in = p["w_in"][l]
    o = 0
    cuts = {}
    for name, wdt in (("a", Q_RANK + KV_RANK + ROPE), ("bqk", GROUP_W), ("bv", GROUP_W), ("bo", GROUP_W),
                      ("bi", N_HEADS), ("bf", N_HEADS), ("c", 2 * GROUP_W), ("d", 2 * GROUP_W)):
        cuts[name] = w_in[:, o:o + wdt]
        o += wdt
    bv = cuts["bv"].reshape(D_MODEL, N_HEADS, HEAD_DIM)
    bv = jnp.pad(bv, ((0, 0), (0, 0), (0, LANE - HEAD_DIM))).reshape(D_MODEL, N_HEADS * LANE)
    w_all = jnp.concatenate([
        _pad_cols(cuts["a"], PA_W), cuts["bqk"], bv, cuts["bo"],
        _pad_cols(cuts["bi"], LANE), _pad_cols(cuts["bf"], LANE), cuts["c"], cuts["d"]], axis=1).astype(BF16)

    def head_pad(w, width):
        return jnp.pad(w, ((0, 0), (0, 0), (0, LANE - width))).reshape(w.shape[0], N_HEADS * LANE)

    wuq = head_pad(p["a_w_uq"][l].reshape(Q_RANK, N_HEADS, QK_DIM), QK_DIM).astype(BF16)
    wukv = p["a_w_ukv"][l].reshape(KV_RANK, N_HEADS, NOPE + HEAD_DIM)
    wk = head_pad(wukv[:, :, :NOPE], NOPE).astype(BF16)
    wv = head_pad(wukv[:, :, NOPE:], HEAD_DIM).astype(BF16)
    qg = _pad_cols(p["a_qn_g"][l][None, :] * (QK_DIM ** -0.5), LANE)
    kg = _pad_cols(p["a_kn_g"][l][None, :], LANE)

    eye = jnp.eye(N_HEADS, dtype=F32)
    wq_bd = jnp.einsum("hde,hg->hdge", p["b_w_q"][l], eye).reshape(GROUP_W, GROUP_W).astype(BF16)
    wk_bd = (jnp.einsum("hde,hg->hdge", p["b_w_k"][l], eye).reshape(GROUP_W, GROUP_W)
             * (HEAD_DIM ** -0.5)).astype(BF16)
    gb = jnp.concatenate([_pad_cols(p["b_b_i"][l][None, :], LANE), _pad_cols(p["b_b_f"][l][None, :], LANE)], axis=1)

    ws_cat = p["c_w_s"][l].transpose(1, 0, 2).reshape(SP_CHUNK, N_HEADS * SP_CHUNK)
    bias_full = jnp.repeat(p["c_b_s"][l].T, HEAD_DIM, axis=1)

    wpq = p["p_w_q"][l].reshape(D_MODEL, P_HEADS, 2, P_HALF).transpose(0, 2, 1, 3).reshape(D_MODEL, 2 * PQ_W)
    keys = p["p_sub_keys"][l]
    eye8 = jnp.eye(P_HEADS, dtype=F32)
    kmat = [jnp.einsum("hnd,hg->nhgd", keys[:, s], eye8).reshape(KROWS, PQ_W).astype(BF16) for s in range(2)]
    return dict(
        norm1_g=p["norm1_g"][l][None, :], w_all=w_all,
        cqg=p["a_cq_g"][l][None, :], ckvg=p["a_ckv_g"][l][None, :], wuq=wuq, wk=wk, wv=wv, qg=qg, kg=kg,
        b_cw=p["b_conv_w"][l], b_cb=p["b_conv_b"][l][None, :], wq_bd=wq_bd, wk_bd=wk_bd, gb=gb,
        hng=p["b_hn_g"][l].reshape(1, GROUP_W),
        c_g=p["c_ln_g"][l][None, :], c_b=p["c_ln_b"][l][None, :], ws_cat=ws_cat, bias_full=bias_full,
        d_w=p["d_dw_w"][l], d_b=p["d_dw_b"][l][None, :],
        d_g=p["d_cn_g"][l].reshape(1, GROUP_W), d_be=p["d_cn_b"][l].reshape(1, GROUP_W),
        w_out=p["w_out"][l].astype(BF16), norm2_g=p["norm2_g"][l][None, :], wpq=wpq.astype(BF16),
        k1=kmat[0], k2=kmat[1], k2h=keys[:, 1].astype(BF16),
    )


def _layer(x, rope, consts, w, tables, layer, bsz, seq):
    t = bsz * seq
    pa, pb, pg, pc, pd = _inproj(x, w["norm1_g"], w["w_all"])
    r3 = lambda a: a.reshape(bsz, seq, a.shape[-1])
    mix_a = _mla(r3(pa), r3(rope[0]), r3(rope[1]), w["cqg"], w["ckvg"], w["wuq"], w["wk"], w["wv"],
                 w["qg"], w["kg"], bsz, seq).reshape(t, GROUP_W)
    mix_b = _mlstm(r3(pb), r3(pg), w["b_cw"], w["b_cb"], w["wq_bd"], w["wk_bd"], w["gb"], w["hng"],
                   consts["tri"], bsz, seq).reshape(t, GROUP_W)
    mix_c = _spatial(pc, w["c_g"], w["c_b"], w["ws_cat"], w["bias_full"])
    mix_d = _convmod(r3(pd), w["d_w"], w["d_b"], consts["avg"], w["d_g"], w["d_be"], bsz, seq).reshape(t, GROUP_W)
    h, xnt, s1, s2, s2h = _outproj(x, mix_a, mix_b, mix_c, mix_d, w["w_out"], w["norm2_g"], w["wpq"],
                                   w["k1"], w["k2"], w["k2h"])
    cnt, wgt, rk, p2h = _route(s1, s2, s2h)
    shp = (P_KEYS, P_HEADS, t)
    return _peer(xnt, tables[0], tables[1], layer, cnt.reshape(shp), wgt.reshape(shp), rk, p2h, h)


def _consts():
    half = ROPE // 2
    inv_freq = ROPE_BASE ** (-jnp.arange(0, ROPE, 2, dtype=F32) / ROPE)
    invf = jnp.zeros((1, LANE), F32)
    invf = invf.at[0, NOPE:NOPE + half].set(inv_freq).at[0, NOPE + half:QK_DIM].set(inv_freq)
    tri = jnp.tril(jnp.ones((ML_L, ML_L), F32)).astype(BF16)
    grp = jnp.arange(GROUP_W) // HEAD_DIM
    avg = ((grp[:, None] == grp[None, :]).astype(F32) / HEAD_DIM).astype(BF16)
    return dict(invf=invf, tri=tri, avg=avg)


def kernel(x, positions, norm1_g, w_in, a_cq_g, a_ckv_g, a_w_uq, a_w_ukv, a_qn_g, a_kn_g, b_conv_w, b_conv_b, b_w_q, b_w_k, b_b_i, b_b_f, b_hn_g, c_ln_g, c_ln_b, c_w_s, c_b_s, d_dw_w, d_dw_b, d_cn_g, d_cn_b, w_out, norm2_g, p_w_q, p_sub_keys, p_u, p_v):
    params = dict(norm1_g=norm1_g, w_in=w_in, a_cq_g=a_cq_g, a_ckv_g=a_ckv_g, a_w_uq=a_w_uq, a_w_ukv=a_w_ukv,
                  a_qn_g=a_qn_g, a_kn_g=a_kn_g, b_conv_w=b_conv_w, b_conv_b=b_conv_b, b_w_q=b_w_q, b_w_k=b_w_k,
                  b_b_i=b_b_i, b_b_f=b_b_f, b_hn_g=b_hn_g, c_ln_g=c_ln_g, c_ln_b=c_ln_b, c_w_s=c_w_s,
                  c_b_s=c_b_s, d_dw_w=d_dw_w, d_dw_b=d_dw_b, d_cn_g=d_cn_g, d_cn_b=d_cn_b, w_out=w_out,
                  norm2_g=norm2_g, p_w_q=p_w_q, p_sub_keys=p_sub_keys, p_u=p_u, p_v=p_v)
    bsz, seq, d = x.shape
    consts = _consts()
    rope = _rope_tables(positions.reshape(bsz * seq, 1), consts["invf"])
    xf = x.reshape(bsz * seq, d)
    tables = (p_u.astype(BF16), p_v.transpose(0, 2, 1).astype(BF16))
    for l in range(norm1_g.shape[0]):
        xf = _layer(xf, rope, consts, _prep_layer(l, params), tables, l, bsz, seq)
    return xf.reshape(bsz, seq, d)
```

```python
import functools

import jax
import jax.numpy as jnp
from jax import lax
from jax.experimental import pallas as pl
from jax.experimental.pallas import tpu as pltpu

F32 = jnp.float32
BF16 = jnp.bfloat16

D_MODEL = 1024
N_HEADS = 4
HEAD_DIM = 64
GROUP_W = N_HEADS * HEAD_DIM
Q_RANK, KV_RANK = 256, 128
NOPE, ROPE = 64, 32
QK_DIM = NOPE + ROPE
ROPE_BASE = 10000.0
LANE = 128
SUBLANES = 8
CONV_B = 4
CONV_D = 31
SP_CHUNK = 128
P_HEADS = 8
P_KEYS = 128
P_TOPK = 16
P_HALF = 64
N_EXPERTS = P_KEYS * P_KEYS
EPS = 1e-6
NEG = -1e30
VMEM_LIMIT = 56 * 1024 * 1024

NT = (((1,), (1,)), ((), ()))


def _cparams(sem, flags=None):
    return pltpu.CompilerParams(dimension_semantics=sem, vmem_limit_bytes=VMEM_LIMIT, flags=flags)


def _full(shape):
    n = len(shape)
    return pl.BlockSpec(shape, lambda *_: (0,) * n)


def _split3(a):
    hi = a.astype(BF16)
    r1 = a - hi.astype(F32)
    mid = r1.astype(BF16)
    lo = (r1 - mid.astype(F32)).astype(BF16)
    return hi, mid, lo


def _dot_exact_rhs(a, m_bf16):
    hi, mid, lo = _split3(a)
    d = lambda t: jnp.dot(t, m_bf16, preferred_element_type=F32)
    return d(hi) + d(mid) + d(lo)


def _dot_exact_lhs(m_bf16, a):
    hi, mid, lo = _split3(a)
    d = lambda t: jnp.dot(m_bf16, t, preferred_element_type=F32)
    return d(hi) + d(mid) + d(lo)


def _sigmoid(x):
    return 1.0 / (1.0 + jnp.exp(-x))


GELU_C1 = 0.7978845608028654
GELU_C2 = 0.044715


def _gelu(x, c1=None, c2=None):
    c = lambda v: jnp.asarray(v, x.dtype)
    c1 = c(GELU_C1) if c1 is None else c1
    c2 = c(GELU_C2) if c2 is None else c2
    inner = (c1 * x) * (c(1.0) + c2 * (x * x))
    return (c(0.5) * x) * (c(1.0) + jnp.tanh(inner))


PA_W, PB_W, PG_W, PC_W, PD_W = 512, 1024, 256, 512, 512
IN_COLS = (0, PA_W, PA_W + PB_W, PA_W + PB_W + PG_W, PA_W + PB_W + PG_W + PC_W,
           PA_W + PB_W + PG_W + PC_W + PD_W)


def _inproj_kernel(x_ref, g_ref, w_ref, pa_ref, pb_ref, pg_ref, pc_ref, pd_ref):
    x = x_ref[...]
    hn = (x * lax.rsqrt(jnp.mean(x * x, axis=-1, keepdims=True) + EPS) * g_ref[...]).astype(BF16)
    outs = (pa_ref, pb_ref, pg_ref, pc_ref, pd_ref)
    for k, o in enumerate(outs):
        y = jnp.dot(hn, w_ref[:, IN_COLS[k]:IN_COLS[k + 1]], preferred_element_type=F32)
        o[...] = y.astype(o.dtype)


def _inproj(x, g, w_all, tm=512):
    t = x.shape[0]
    widths = (PA_W, PB_W, PG_W, PC_W, PD_W)
    dts = (BF16, BF16, F32, BF16, BF16)
    return pl.pallas_call(
        _inproj_kernel,
        grid=(t // tm,),
        in_specs=[pl.BlockSpec((tm, D_MODEL), lambda i: (i, 0)),
                  _full((1, D_MODEL)), _full((D_MODEL, IN_COLS[-1]))],
        out_specs=[pl.BlockSpec((tm, w), lambda i: (i, 0)) for w in widths],
        out_shape=[jax.ShapeDtypeStruct((t, w), d) for w, d in zip(widths, dts)],
        compiler_params=_cparams(("parallel",)),
        name="inproj",
    )(x, g, w_all)


MLA_TQ = 256
MLA_PRO = 512


def _rope_kernel(pos_ref, invf_ref, cos_ref, sin_ref):
    ang = pos_ref[...].astype(F32) * invf_ref[...]
    cos_ref[...] = jnp.cos(ang)
    sin_ref[...] = jnp.sin(ang)


def _rope_tables(pos, invf, tm=1024):
    t = pos.shape[0]
    blk = pl.BlockSpec((tm, LANE), lambda i: (i, 0))
    return pl.pallas_call(
        _rope_kernel,
        grid=(t // tm,),
        in_specs=[pl.BlockSpec((tm, 1), lambda i: (i, 0)), _full((1, LANE))],
        out_specs=[blk, blk],
        out_shape=[jax.ShapeDtypeStruct((t, LANE), F32)] * 2,
        compiler_params=_cparams(("parallel",)),
        name="rope",
    )(pos, invf)


def _mla_kernel(pa_ref, cos_ref, sin_ref, cqg_ref, ckvg_ref, wuq_ref, wk_ref, wv_ref,
                qg_ref, kg_ref, o_ref, q_s, k_s, v_s, *, seq):
    tq = MLA_TQ

    def chunk(c, carry):
        r0 = pl.multiple_of(c * MLA_PRO, MLA_PRO)
        pa = pa_ref[0, pl.ds(r0, MLA_PRO), :]
        cq = pa[:, 0:Q_RANK].astype(F32)
        ckv = pa[:, Q_RANK:Q_RANK + KV_RANK].astype(F32)
        krp = pa[:, Q_RANK + KV_RANK:].astype(F32)
        cqn = (cq * lax.rsqrt(jnp.mean(cq * cq, -1, keepdims=True) + EPS) * cqg_ref[...]).astype(BF16)
        ckvn = (ckv * lax.rsqrt(jnp.mean(ckv * ckv, -1, keepdims=True) + EPS) * ckvg_ref[...]).astype(BF16)
        q = jnp.dot(cqn, wuq_ref[...], preferred_element_type=F32)
        kn = jnp.dot(ckvn, wk_ref[...], preferred_element_type=F32)
        v = jnp.dot(ckvn, wv_ref[...], preferred_element_type=F32)
        lane = lax.broadcasted_iota(jnp.int32, (MLA_PRO, LANE), 1)
        cosv = cos_ref[0, pl.ds(r0, MLA_PRO), :]
        sinv = sin_ref[0, pl.ds(r0, MLA_PRO), :]
        half = ROPE // 2
        sin_hi = jnp.where((lane >= NOPE + half) & (lane < QK_DIM), sinv, 0.0)
        sin_lo = jnp.where((lane >= NOPE) & (lane < NOPE + half), -sinv, 0.0)
        kr_sh = pltpu.roll(krp, NOPE, 1)

        def norm_rope(t, g):
            t = t * lax.rsqrt(jnp.sum(t * t, -1, keepdims=True) * (1.0 / QK_DIM) + EPS) * g
            return (t * cosv + pltpu.roll(t, half, 1) * sin_hi
                    + pltpu.roll(t, LANE - half, 1) * sin_lo)

        for h in range(N_HEADS):
            sl = slice(h * LANE, (h + 1) * LANE)
            q_s[pl.ds(r0, MLA_PRO), sl] = norm_rope(q[:, sl], qg_ref[...]).astype(BF16)
            k_s[pl.ds(r0, MLA_PRO), sl] = norm_rope(kn[:, sl] + kr_sh, kg_ref[...]).astype(BF16)
            v_s[pl.ds(r0, MLA_PRO), sl] = jnp.where(lane == HEAD_DIM, 1.0, v[:, sl]).astype(BF16)
        return carry

    lax.fori_loop(0, seq // MLA_PRO, chunk, 0)

    row = lax.broadcasted_iota(jnp.int32, (tq, tq), 0)
    col = lax.broadcasted_iota(jnp.int32, (tq, tq), 1)
    for i in range(seq // tq):
        klen = (i + 1) * tq
        for h in range(N_HEADS):
            sl = slice(h * LANE, (h + 1) * LANE)
            qh = q_s[i * tq:(i + 1) * tq, sl]
            s = lax.dot_general(qh, k_s[0:klen, sl], NT, preferred_element_type=F32)
            diag = jnp.where(col <= row, s[:, klen - tq:], NEG)
            s = diag if i == 0 else jnp.concatenate([s[:, :klen - tq], diag], axis=1)
            p = jnp.exp(s - jnp.max(s, -1, keepdims=True)).astype(BF16)
            acc = jnp.dot(p, v_s[0:klen, sl], preferred_element_type=F32)
            out_h = acc[:, :HEAD_DIM] / acc[:, HEAD_DIM:HEAD_DIM + 1]
            o_ref[0, i * tq:(i + 1) * tq, h * HEAD_DIM:(h + 1) * HEAD_DIM] = out_h.astype(BF16)


def _mla(pa, cosv, sinv, cqg, ckvg, wuq, wk, wv, qg, kg, bsz, seq):
    hw = N_HEADS * LANE
    seq_blk = lambda w: pl.BlockSpec((1, seq, w), lambda b: (b, 0, 0))
    return pl.pallas_call(
        functools.partial(_mla_kernel, seq=seq),
        grid=(bsz,),
        in_specs=[seq_blk(PA_W), seq_blk(LANE), seq_blk(LANE),
                  _full((1, Q_RANK)), _full((1, KV_RANK)),
                  _full((Q_RANK, hw)), _full((KV_RANK, hw)), _full((KV_RANK, hw)),
                  _full((1, LANE)), _full((1, LANE))],
        out_specs=seq_blk(GROUP_W),
        out_shape=jax.ShapeDtypeStruct((bsz, seq, GROUP_W), BF16),
        scratch_shapes=[pltpu.VMEM((seq, hw), BF16)] * 3,
        compiler_params=_cparams(("parallel",)),
        name="mla",
    )(pa, cosv, sinv, cqg, ckvg, wuq, wk, wv, qg, kg)


ML_L = 256
ML_HALO = 8
ML_NB = 2


def _mlstm_kernel(pb_ref, pg_ref, cw_ref, cb_ref, wq_ref, wk_ref, gb_ref, hng_ref, tri_ref,
                  o_ref, xpad, st, mst, fst):
    for bi in range(ML_NB):
        _mlstm_chunk(pb_ref.at[bi], pg_ref.at[bi], cw_ref, cb_ref, wq_ref, wk_ref, gb_ref, hng_ref, tri_ref,
                     o_ref.at[bi], xpad.at[bi], st.at[bi], mst.at[bi], fst.at[bi])


def _mlstm_chunk(pb_ref, pg_ref, cw_ref, cb_ref, wq_ref, wk_ref, gb_ref, hng_ref, tri_ref,
                 o_ref, xpad, st, mst, fst):
    c = pl.program_id(1)
    L = ML_L

    @pl.when(c == 0)
    def _init():
        xpad[0:ML_HALO, :] = jnp.zeros((ML_HALO, GROUP_W), F32)
        st[...] = jnp.zeros(st.shape, F32)
        mst[...] = jnp.full(mst.shape, NEG, F32)
        fst[...] = jnp.zeros(fst.shape, F32)

    xqk = pb_ref[:, 0:GROUP_W].astype(F32)
    xpad[ML_HALO:ML_HALO + L, :] = xqk
    acc = cb_ref[...] + jnp.zeros((L, GROUP_W), F32)
    for j in range(CONV_B):
        off = ML_HALO - (CONV_B - 1) + j
        acc = acc + cw_ref[j:j + 1, :] * xpad[off:off + L, :]
    xpad[0:ML_HALO, :] = xqk[L - ML_HALO:, :]
    xc = (acc * _sigmoid(acc)).astype(BF16)
    q = jnp.dot(xc, wq_ref[...], preferred_element_type=F32)
    k = jnp.dot(xc, wk_ref[...], preferred_element_type=F32)

    gates = pg_ref[...] + gb_ref[...]
    ig = gates[:, :LANE]
    fg = gates[:, LANE:]
    lf = jnp.minimum(fg, 0.0) - jnp.log(1.0 + jnp.exp(-jnp.abs(fg)))
    fcol = _dot_exact_lhs(tri_ref[...], lf) + fst[...]
    fst[...] = fcol[L - 1:L, :]
    a_col = ig - fcol
    a_t = a_col.T

    rowi = lax.broadcasted_iota(jnp.int32, (L, L), 0)
    coli = lax.broadcasted_iota(jnp.int32, (L, L), 1)
    causal = coli <= rowi
    lane_v = lax.broadcasted_iota(jnp.int32, (L, LANE), 1)
    lane_w = lax.broadcasted_iota(jnp.int32, (L, GROUP_W), 1)

    mts, mns, mcs = [], [], []
    for h in range(N_HEADS):
        amat = jnp.where(causal, a_t[h:h + 1, :], NEG)
        mc = mst[:, h:h + 1]
        mt = jnp.maximum(jnp.max(amat, -1, keepdims=True), mc)
        mts.append((amat, mt))
        mcs.append(mc)
        mns.append(mt[L - 1:L, :])

    wg = jnp.exp(a_col[:, N_HEADS - 1:N_HEADS] - mns[N_HEADS - 1])
    for h in range(N_HEADS - 2, -1, -1):
        wg = jnp.where(lane_w < (h + 1) * HEAD_DIM, jnp.exp(a_col[:, h:h + 1] - mns[h]), wg)
    kw_t = (k * wg).T.astype(BF16)

    for h in range(N_HEADS):
        hs = slice(h * HEAD_DIM, (h + 1) * HEAD_DIM)
        amat, mt = mts[h]
        mc, mn = mcs[h], mns[h]
        p = jnp.exp(amat - mt)
        qh = q[:, hs].astype(BF16)
        kh = k[:, hs].astype(BF16)
        qk = lax.dot_general(qh, kh, NT, preferred_element_type=F32)
        w = (p * qk).astype(BF16)
        vext = jnp.where(lane_v == HEAD_DIM, 1.0,
                         pb_ref[:, GROUP_W + h * LANE:GROUP_W + (h + 1) * LANE].astype(F32)).astype(BF16)
        sth = st[h]
        nd = (jnp.dot(w, vext, preferred_element_type=F32)
              + jnp.exp(mc - mt) * jnp.dot(qh, sth.astype(BF16), preferred_element_type=F32))
        den = nd[:, HEAD_DIM:HEAD_DIM + 1]
        floor = jnp.exp(-(fcol[:, h:h + 1] + mt))
        hh = nd[:, :HEAD_DIM] / jnp.maximum(jnp.abs(den), floor)
        hh = hh * lax.rsqrt(jnp.mean(hh * hh, -1, keepdims=True) + EPS) * hng_ref[:, hs]
        xo = pb_ref[:, GROUP_W + N_HEADS * LANE + h * HEAD_DIM:
                    GROUP_W + N_HEADS * LANE + (h + 1) * HEAD_DIM].astype(F32)
        o_ref[:, hs] = (_sigmoid(xo) * hh).astype(BF16)
        st[h] = jnp.exp(mc - mn) * sth + jnp.dot(kw_t[hs, :], vext, preferred_element_type=F32)
        mst[:, h:h + 1] = mn


def _mlstm(pb, pg, cw, cb, wq, wk, gb, hng, tri, bsz, seq):
    L = ML_L
    return pl.pallas_call(
        _mlstm_kernel,
        grid=(bsz // ML_NB, seq // L),
        in_specs=[pl.BlockSpec((ML_NB, L, PB_W), lambda b, c: (b, c, 0)),
                  pl.BlockSpec((ML_NB, L, PG_W), lambda b, c: (b, c, 0)),
                  _full((CONV_B, GROUP_W)), _full((1, GROUP_W)),
                  _full((GROUP_W, GROUP_W)), _full((GROUP_W, GROUP_W)),
                  _full((1, PG_W)), _full((1, GROUP_W)), _full((L, L))],
        out_specs=pl.BlockSpec((ML_NB, L, GROUP_W), lambda b, c: (b, c, 0)),
        out_shape=jax.ShapeDtypeStruct((bsz, seq, GROUP_W), BF16),
        scratch_shapes=[pltpu.VMEM((ML_NB, L + ML_HALO, GROUP_W), F32),
                        pltpu.VMEM((ML_NB, N_HEADS, HEAD_DIM, LANE), F32),
                        pltpu.VMEM((ML_NB, 1, LANE), F32),
                        pltpu.VMEM((ML_NB, 1, LANE), F32)],
        compiler_params=_cparams(("parallel", "arbitrary")),
        name="mlstm",
    )(pb, pg, cw, cb, wq, wk, gb, hng, tri)


SP_STEP = 4


def _spatial_kernel(pc_ref, g_ref, b_ref, ws_ref, bias_ref, o_ref):
    T = SP_CHUNK
    rowi = lax.broadcasted_iota(jnp.int32, (T, N_HEADS * T), 0)
    coli = lax.broadcasted_iota(jnp.int32, (T, N_HEADS * T), 1)
    ws = jnp.where((coli & (T - 1)) <= rowi, ws_ref[...], 0.0).astype(BF16)
    rgrp = lax.broadcasted_iota(jnp.int32, (T, GROUP_W), 1) // HEAD_DIM
    for cidx in range(SP_STEP):
        rs = slice(cidx * T, (cidx + 1) * T)
        u = _gelu(pc_ref[rs, 0:GROUP_W].astype(F32))
        gv = _gelu(pc_ref[rs, GROUP_W:].astype(F32))
        mu = jnp.mean(gv, -1, keepdims=True)
        dv = gv - mu
        var = jnp.mean(dv * dv, -1, keepdims=True)
        vn = (dv * lax.rsqrt(var + EPS) * g_ref[...] + b_ref[...]).astype(BF16)
        zero = jnp.zeros_like(vn)
        vbig = jnp.concatenate([jnp.where(rgrp == g, vn, zero) for g in range(N_HEADS)], axis=0)
        sg = jnp.dot(ws, vbig, preferred_element_type=F32) + bias_ref[...]
        o_ref[rs, :] = (u * sg).astype(BF16)


def _spatial(pc, g, b, ws_cat, bias_full):
    t = pc.shape[0]
    tm = SP_STEP * SP_CHUNK
    return pl.pallas_call(
        _spatial_kernel,
        grid=(t // tm,),
        in_specs=[pl.BlockSpec((tm, PC_W), lambda i: (i, 0)),
                  _full((1, GROUP_W)), _full((1, GROUP_W)),
                  _full((SP_CHUNK, N_HEADS * SP_CHUNK)), _full((SP_CHUNK, GROUP_W))],
        out_specs=pl.BlockSpec((tm, GROUP_W), lambda i: (i, 0)),
        out_shape=jax.ShapeDtypeStruct((t, GROUP_W), BF16),
        compiler_params=_cparams(("parallel",)),
        name="spatial",
    )(pc, g, b, ws_cat, bias_full)


CV_T = 256
CV_HALO = 32


def _convmod_kernel(pd_ref, w_ref, b_ref, avg_ref, g_ref, be_ref, o_ref, ypad, zsh):
    i = pl.program_id(1)
    T = CV_T

    @pl.when(i == 0)
    def _init():
        ypad[0:CV_HALO, :] = jnp.zeros((CV_HALO, GROUP_W), F32)

    a = pd_ref[0, :, 0:GROUP_W].astype(F32)
    b = pd_ref[0, :, GROUP_W:].astype(F32)
    y = a * _sigmoid(b)
    ypad[CV_HALO:CV_HALO + T, :] = y
    acc = b_ref[...] + jnp.zeros((T, GROUP_W), F32)
    off0 = CV_HALO - (CONV_D - 1)
    for b in range(SUBLANES):
        offs = [o for o in range(off0, off0 + CONV_D) if o % SUBLANES == b]
        span = offs[-1] - b + T
        src = ypad
        if b:
            zsh[0:span, :] = ypad[b:b + span, :]
            src = zsh
        for o in offs:
            acc = acc + w_ref[o - off0:o - off0 + 1, :] * src[o - b:o - b + T, :]
    ypad[0:CV_HALO, :] = y[T - CV_HALO:, :]
    mu = _dot_exact_rhs(acc, avg_ref[...])
    dv = acc - mu
    var = _dot_exact_rhs(dv * dv, avg_ref[...])
    yn = dv * lax.rsqrt(var + EPS) * g_ref[...] + be_ref[...]
    o_ref[0] = (yn * _sigmoid(yn)).astype(BF16)


def _convmod(pd, w, b, avg, g, be, bsz, seq):
    return pl.pallas_call(
        _convmod_kernel,
        grid=(bsz, seq // CV_T),
        in_specs=[pl.BlockSpec((1, CV_T, PD_W), lambda bb, i: (bb, i, 0)),
                  _full((CONV_D, GROUP_W)), _full((1, GROUP_W)), _full((GROUP_W, GROUP_W)),
                  _full((1, GROUP_W)), _full((1, GROUP_W))],
        out_specs=pl.BlockSpec((1, CV_T, GROUP_W), lambda bb, i: (bb, i, 0)),
        out_shape=jax.ShapeDtypeStruct((bsz, seq, GROUP_W), BF16),
        scratch_shapes=[pltpu.VMEM((CV_T + CV_HALO, GROUP_W), F32)] * 2,
        compiler_params=_cparams(("parallel", "arbitrary")),
        name="convmod",
    )(pd, w, b, avg, g, be)


OP_TM = 512
PQ_W = P_HEADS * P_HALF
KROWS = P_KEYS * P_HEADS


def _outproj_kernel(x_ref, ma_ref, mb_ref, mc_ref, md_ref, wo_ref, g_ref, wpq_ref, k1_ref, k2_ref,
                    h_ref, xnt_ref, s1_ref, s2_ref):
    acc = x_ref[...]
    for g, m in enumerate((ma_ref, mb_ref, mc_ref, md_ref)):
        acc = acc + jnp.dot(m[...], wo_ref[g * GROUP_W:(g + 1) * GROUP_W, :], preferred_element_type=F32)
    h_ref[...] = acc
    hn = acc * lax.rsqrt(jnp.mean(acc * acc, -1, keepdims=True) + EPS) * g_ref[...]
    xnt_ref[...] = hn.T.astype(BF16)
    q = jnp.dot(hn.astype(BF16), wpq_ref[...], preferred_element_type=F32).astype(BF16)
    s1_ref[...] = lax.dot_general(k1_ref[...], q[:, :PQ_W], NT, preferred_element_type=F32)
    s2_ref[...] = lax.dot_general(k2_ref[...], q[:, PQ_W:], NT, preferred_element_type=F32)


def _outproj(x, ma, mb, mc, md, wo, g, wpq, k1, k2):
    t = x.shape[0]
    tm = OP_TM
    mix = pl.BlockSpec((tm, GROUP_W), lambda i: (i, 0))
    tcol = pl.BlockSpec((KROWS, tm), lambda i: (0, i))
    return pl.pallas_call(
        _outproj_kernel,
        grid=(t // tm,),
        in_specs=[pl.BlockSpec((tm, D_MODEL), lambda i: (i, 0)), mix, mix, mix, mix,
                  _full((D_MODEL, D_MODEL)), _full((1, D_MODEL)), _full((D_MODEL, 2 * PQ_W)),
                  _full((KROWS, PQ_W)), _full((KROWS, PQ_W))],
        out_specs=[pl.BlockSpec((tm, D_MODEL), lambda i: (i, 0)),
                   pl.BlockSpec((D_MODEL, tm), lambda i: (0, i)), tcol, tcol],
        out_shape=[jax.ShapeDtypeStruct((t, D_MODEL), F32),
                   jax.ShapeDtypeStruct((D_MODEL, t), BF16),
                   jax.ShapeDtypeStruct((KROWS, t), F32),
                   jax.ShapeDtypeStruct((KROWS, t), F32)],
        compiler_params=_cparams(("parallel",)),
        name="outproj",
    )(x, ma, mb, mc, md, wo, g, wpq, k1, k2)


RT_T = LANE
BF_ROWS = 16


def _bitonic_merge(v):
    n = len(v)
    if n == 1:
        return v
    half = n // 2
    hi = [jnp.maximum(v[i], v[i + half]) for i in range(half)]
    lo = [jnp.minimum(v[i], v[i + half]) for i in range(half)]
    return _bitonic_merge(hi) + _bitonic_merge(lo)


def _bitonic_sort(v):
    n = len(v)
    if n == 1:
        return v
    return _bitonic_merge(_bitonic_sort(v[:n // 2]) + _bitonic_sort(v[n // 2:])[::-1])


def _merge_top(a, b):
    n = len(a)
    c = list(a)
    for k, bv in enumerate(b):
        c[n - 1 - k] = jnp.maximum(a[n - 1 - k], bv)
    return _bitonic_merge(c)


def _sorted_top(ref, lanes):
    top = None
    for grp in range(P_KEYS // P_TOPK):
        tiles = [ref[(grp * P_TOPK + k) * P_HEADS:(grp * P_TOPK + k + 1) * P_HEADS, lanes]
                 for k in range(P_TOPK)]
        srt = _bitonic_sort(tiles)
        top = srt if top is None else _merge_top(top, srt)
    return top


def _route_kernel(s1_ref, s2_ref, cnt_ref, w_ref, rk_ref, p2h_ref):
    for g in range(RT_T // LANE):
        lanes = slice(g * LANE, (g + 1) * LANE)
        t1 = _sorted_top(s1_ref, lanes)
        t2 = _sorted_top(s2_ref, lanes)
        rows = [[t1[a] + t2[b] for b in range(P_TOPK // (a + 1))] for a in range(P_TOPK)]
        top = rows[0]
        for a in range(1, P_TOPK):
            top = _merge_top(top, rows[a])
        tau = top[P_TOPK - 1]
        cmax = rows[0][0]
        z = jnp.zeros((P_HEADS, LANE), F32)
        cnt_a = []
        for a in range(P_TOPK):
            ca = jnp.zeros((P_HEADS, LANE), F32)
            for cnd in rows[a]:
                sel = cnd >= tau
                z = z + jnp.where(sel, jnp.exp(cnd - cmax), 0.0)
                ca = ca + jnp.where(sel, 1.0, 0.0)
            cnt_a.append(ca)
        zinv = 1.0 / z

        def body(n, carry):
            r0 = pl.multiple_of(n * P_HEADS, P_HEADS)
            s1 = s1_ref[pl.ds(r0, P_HEADS), lanes]
            cn = jnp.where(s1 + t2[0] >= tau, 1.0, 0.0)
            for a in range(P_TOPK // 2 - 1, -1, -1):
                cn = jnp.where(s1 == t1[a], cnt_a[a], cn)
            cnt_ref[pl.ds(r0, P_HEADS), lanes] = cn
            w_ref[pl.ds(r0, P_HEADS), lanes] = jnp.exp(s1 - t1[0]) * zinv
            return carry

        lax.fori_loop(0, P_KEYS, body, 0, unroll=4)

        for h in range(P_HEADS):
            tb = [jnp.broadcast_to(t2[b][h:h + 1, :], (BF_ROWS, LANE)) for b in range(P_TOPK)]

            def rbody(r, carry, h=h, tb=tb):
                r0 = pl.multiple_of(h * P_KEYS + r * BF_ROWS, BF_ROWS)
                s2 = s2_ref[pl.ds(h + r * (BF_ROWS * P_HEADS), BF_ROWS, stride=P_HEADS), lanes]
                rk = jnp.full((BF_ROWS, LANE), float(P_TOPK), F32)
                for b in range(P_TOPK - 1, -1, -1):
                    rk = jnp.where(tb[b] > s2, rk, float(b))
                rk_ref[pl.ds(r0, BF_ROWS), lanes] = rk.astype(BF16)
                p2h_ref[pl.ds(r0, BF_ROWS), lanes] = jnp.exp(s2 - tb[0]).astype(BF16)
                return carry

            lax.fori_loop(0, P_KEYS // BF_ROWS, rbody, 0, unroll=2)


def _route(s1, s2):
    t = s1.shape[1]
    blk = pl.BlockSpec((KROWS, RT_T), lambda i: (0, i))
    return pl.pallas_call(
        _route_kernel,
        grid=(t // RT_T,),
        in_specs=[blk, blk],
        out_specs=[blk, blk, blk, blk],
        out_shape=[jax.ShapeDtypeStruct((KROWS, t), F32), jax.ShapeDtypeStruct((KROWS, t), F32),
                   jax.ShapeDtypeStruct((KROWS, t), BF16), jax.ShapeDtypeStruct((KROWS, t), BF16)],
        compiler_params=_cparams(("parallel",)),
        name="route",
    )(s1, s2)


PE_TT = 512
PE_EB = 1024
PE_K1 = PE_EB // P_KEYS


PE_GL = 256


def _peer_kernel(xnt_ref, u_ref, vtp_ref, vtl_ref, cnt_ref, w_ref, rk_ref, p2h_ref, h_ref, gc_ref, o_ref,
                 acc, ht, gsc, cnt_s, w_s, rk_s, p2_s):
    j = pl.program_id(1)
    slot = j % 2

    @pl.when(j == 0)
    def _init():
        acc[...] = jnp.zeros(acc.shape, F32)
        ht[1] = jnp.zeros(ht.shape[1:], BF16)
        rk_s[...] = rk_ref[...]
        p2_s[...] = p2h_ref[...]

    e0 = j * PE_K1
    cnt_s[...] = cnt_ref[pl.ds(e0, PE_K1)]
    w_s[...] = w_ref[pl.ds(e0, PE_K1)]

    ngrp = P_KEYS // BF_ROWS
    bzero = jnp.zeros((), BF16)
    gtile = lambda r: jnp.concatenate(
        [jnp.broadcast_to(gc_ref[r:r + 1, :], (BF_ROWS, PE_GL)).astype(BF16)] * (PE_EB // BF_ROWS), axis=0)
    gc1, gc2 = gtile(0), gtile(1)
    for s in range(PE_TT // PE_GL):
        ls = slice(s * PE_GL, (s + 1) * PE_GL)
        acc[:, ls] += jnp.dot(vtp_ref[...], ht[1 - slot, :, ls], preferred_element_type=F32)
        for c in range(PE_K1):
            gate = [jnp.zeros((BF_ROWS, PE_GL), BF16) for _ in range(ngrp)]
            for h in range(P_HEADS):
                cn = jnp.broadcast_to(cnt_s[c, h:h + 1, ls], (BF_ROWS, PE_GL)).astype(BF16)
                wg = jnp.broadcast_to(w_s[c, h:h + 1, ls], (BF_ROWS, PE_GL)).astype(BF16)
                for r in range(ngrp):
                    rows = slice(h * P_KEYS + r * BF_ROWS, h * P_KEYS + (r + 1) * BF_ROWS)
                    sel = jnp.where(rk_s[rows, ls] < cn, p2_s[rows, ls], bzero)
                    gate[r] = gate[r] + sel * wg
            for r in range(ngrp):
                rows = slice(c * P_KEYS + r * BF_ROWS, c * P_KEYS + (r + 1) * BF_ROWS)
                gsc[rows, ls] = gate[r]
        a = jnp.dot(u_ref[...], xnt_ref[:, ls], preferred_element_type=F32)
        ht[slot, :, ls] = _gelu(a.astype(BF16), gc1, gc2) * gsc[:, ls]

    @pl.when(j == pl.num_programs(1) - 1)
    def _fin():
        last = jnp.dot(vtl_ref[...], ht[slot], preferred_element_type=F32)
        o_ref[...] = h_ref[...] + (acc[...] + last).T


def _peer(xnt, u_all, vt_all, layer, cnt, w, rk, p2h, hres):
    t = xnt.shape[1]
    nj = N_EXPERTS // PE_EB
    gconst = jnp.zeros((SUBLANES, PE_GL), F32).at[0].set(GELU_C1).at[1].set(GELU_C2)
    tok = lambda r: pl.BlockSpec((r, PE_TT), lambda i, j: (0, i))
    tok3 = pl.BlockSpec((P_KEYS, P_HEADS, PE_TT), lambda i, j: (0, 0, i))
    return pl.pallas_call(
        _peer_kernel,
        grid=(t // PE_TT, nj),
        in_specs=[tok(D_MODEL),
                  pl.BlockSpec((None, PE_EB, D_MODEL), lambda i, j: (layer, j, 0)),
                  pl.BlockSpec((None, D_MODEL, PE_EB), lambda i, j: (layer, 0, jnp.maximum(j - 1, 0))),
                  pl.BlockSpec((None, D_MODEL, PE_EB), lambda i, j: (layer, 0, nj - 1)),
                  tok3, tok3, tok(KROWS), tok(KROWS),
                  pl.BlockSpec((PE_TT, D_MODEL), lambda i, j: (i, 0)),
                  _full((SUBLANES, PE_GL))],
        out_specs=pl.BlockSpec((PE_TT, D_MODEL), lambda i, j: (i, 0)),
        out_shape=jax.ShapeDtypeStruct((t, D_MODEL), F32),
        scratch_shapes=[pltpu.VMEM((D_MODEL, PE_TT), F32), pltpu.VMEM((2, PE_EB, PE_TT), BF16),
                        pltpu.VMEM((PE_EB, PE_TT), BF16),
                        pltpu.VMEM((PE_K1, P_HEADS, PE_TT), F32), pltpu.VMEM((PE_K1, P_HEADS, PE_TT), F32),
                        pltpu.VMEM((KROWS, PE_TT), BF16), pltpu.VMEM((KROWS, PE_TT), BF16)],
        compiler_params=_cparams(("parallel", "arbitrary")),
        name="peer",
    )(xnt, u_all, vt_all, vt_all, cnt, w, rk, p2h, hres, gconst)


def _pad_cols(w, width):
    return jnp.pad(w, ((0, 0), (0, width - w.shape[1])))


def _prep_layer(l, p):
    w_in = p["w_in"][l]
    o = 0
    cuts = {}
    for name, wdt in (("a", Q_RANK + KV_RANK + ROPE), ("bqk", GROUP_W), ("bv", GROUP_W), ("bo", GROUP_W),
                      ("bi", N_HEADS), ("bf", N_HEADS), ("c", 2 * GROUP_W), ("d", 2 * GROUP_W)):
        cuts[name] = w_in[:, o:o + wdt]
        o += wdt
    bv = cuts["bv"].reshape(D_MODEL, N_HEADS, HEAD_DIM)
    bv = jnp.pad(bv, ((0, 0), (0, 0), (0, LANE - HEAD_DIM))).reshape(D_MODEL, N_HEADS * LANE)
    w_all = jnp.concatenate([
        _pad_cols(cuts["a"], PA_W), cuts["bqk"], bv, cuts["bo"],
        _pad_cols(cuts["bi"], LANE), _pad_cols(cuts["bf"], LANE), cuts["c"], cuts["d"]], axis=1).astype(BF16)

    def head_pad(w, width):
        return jnp.pad(w, ((0, 0), (0, 0), (0, LANE - width))).reshape(w.shape[0], N_HEADS * LANE)

    wuq = head_pad(p["a_w_uq"][l].reshape(Q_RANK, N_HEADS, QK_DIM), QK_DIM).astype(BF16)
    wukv = p["a_w_ukv"][l].reshape(KV_RANK, N_HEADS, NOPE + HEAD_DIM)
    wk = head_pad(wukv[:, :, :NOPE], NOPE).astype(BF16)
    wv = head_pad(wukv[:, :, NOPE:], HEAD_DIM).astype(BF16)
    qg = _pad_cols(p["a_qn_g"][l][None, :] * (QK_DIM ** -0.5), LANE)
    kg = _pad_cols(p["a_kn_g"][l][None, :], LANE)

    eye = jnp.eye(N_HEADS, dtype=F32)
    wq_bd = jnp.einsum("hde,hg->hdge", p["b_w_q"][l], eye).reshape(GROUP_W, GROUP_W).astype(BF16)
    wk_bd = (jnp.einsum("hde,hg->hdge", p["b_w_k"][l], eye).reshape(GROUP_W, GROUP_W)
             * (HEAD_DIM ** -0.5)).astype(BF16)
    gb = jnp.concatenate([_pad_cols(p["b_b_i"][l][None, :], LANE), _pad_cols(p["b_b_f"][l][None, :], LANE)], axis=1)

    ws_cat = p["c_w_s"][l].transpose(1, 0, 2).reshape(SP_CHUNK, N_HEADS * SP_CHUNK)
    bias_full = jnp.repeat(p["c_b_s"][l].T, HEAD_DIM, axis=1)

    wpq = p["p_w_q"][l].reshape(D_MODEL, P_HEADS, 2, P_HALF).transpose(0, 2, 1, 3).reshape(D_MODEL, 2 * PQ_W)
    keys = p["p_sub_keys"][l]
    eye8 = jnp.eye(P_HEADS, dtype=F32)
    kmat = [jnp.einsum("hnd,hg->nhgd", keys[:, s], eye8).reshape(KROWS, PQ_W).astype(BF16) for s in range(2)]
    return dict(
        norm1_g=p["norm1_g"][l][None, :], w_all=w_all,
        cqg=p["a_cq_g"][l][None, :], ckvg=p["a_ckv_g"][l][None, :], wuq=wuq, wk=wk, wv=wv, qg=qg, kg=kg,
        b_cw=p["b_conv_w"][l], b_cb=p["b_conv_b"][l][None, :], wq_bd=wq_bd, wk_bd=wk_bd, gb=gb,
        hng=p["b_hn_g"][l].reshape(1, GROUP_W),
        c_g=p["c_ln_g"][l][None, :], c_b=p["c_ln_b"][l][None, :], ws_cat=ws_cat, bias_full=bias_full,
        d_w=p["d_dw_w"][l], d_b=p["d_dw_b"][l][None, :],
        d_g=p["d_cn_g"][l].reshape(1, GROUP_W), d_be=p["d_cn_b"][l].reshape(1, GROUP_W),
        w_out=p["w_out"][l].astype(BF16), norm2_g=p["norm2_g"][l][None, :], wpq=wpq.astype(BF16),
        k1=kmat[0], k2=kmat[1],
    )


def _layer(x, rope, consts, w, tables, layer, bsz, seq):
    t = bsz * seq
    pa, pb, pg, pc, pd = _inproj(x, w["norm1_g"], w["w_all"])
    r3 = lambda a: a.reshape(bsz, seq, a.shape[-1])
    mix_a = _mla(r3(pa), r3(rope[0]), r3(rope[1]), w["cqg"], w["ckvg"], w["wuq"], w["wk"], w["wv"],
                 w["qg"], w["kg"], bsz, seq).reshape(t, GROUP_W)
    mix_b = _mlstm(r3(pb), r3(pg), w["b_cw"], w["b_cb"], w["wq_bd"], w["wk_bd"], w["gb"], w["hng"],
                   consts["tri"], bsz, seq).reshape(t, GROUP_W)
    mix_c = _spatial(pc, w["c_g"], w["c_b"], w["ws_cat"], w["bias_full"])
    mix_d = _convmod(r3(pd), w["d_w"], w["d_b"], consts["avg"], w["d_g"], w["d_be"], bsz, seq).reshape(t, GROUP_W)
    h, xnt, s1, s2 = _outproj(x, mix_a, mix_b, mix_c, mix_d, w["w_out"], w["norm2_g"], w["wpq"],
                              w["k1"], w["k2"])
    cnt, wgt, rk, p2h = _route(s1, s2)
    shp = (P_KEYS, P_HEADS, t)
    return _peer(xnt, tables[0], tables[1], layer, cnt.reshape(shp), wgt.reshape(shp), rk, p2h, h)


def _consts():
    half = ROPE // 2
    inv_freq = ROPE_BASE ** (-jnp.arange(0, ROPE, 2, dtype=F32) / ROPE)
    invf = jnp.zeros((1, LANE), F32)
    invf = invf.at[0, NOPE:NOPE + half].set(inv_freq).at[0, NOPE + half:QK_DIM].set(inv_freq)
    tri = jnp.tril(jnp.ones((ML_L, ML_L), F32)).astype(BF16)
    grp = jnp.arange(GROUP_W) // HEAD_DIM
    avg = ((grp[:, None] == grp[None, :]).astype(F32) / HEAD_DIM).astype(BF16)
    return dict(invf=invf, tri=tri, avg=avg)


def kernel(x, positions, norm1_g, w_in, a_cq_g, a_ckv_g, a_w_uq, a_w_ukv, a_qn_g, a_kn_g, b_conv_w, b_conv_b, b_w_q, b_w_k, b_b_i, b_b_f, b_hn_g, c_ln_g, c_ln_b, c_w_s, c_b_s, d_dw_w, d_dw_b, d_cn_g, d_cn_b, w_out, norm2_g, p_w_q, p_sub_keys, p_u, p_v):
    params = dict(norm1_g=norm1_g, w_in=w_in, a_cq_g=a_cq_g, a_ckv_g=a_ckv_g, a_w_uq=a_w_uq, a_w_ukv=a_w_ukv,
                  a_qn_g=a_qn_g, a_kn_g=a_kn_g, b_conv_w=b_conv_w, b_conv_b=b_conv_b, b_w_q=b_w_q, b_w_k=b_w_k,
                  b_b_i=b_b_i, b_b_f=b_b_f, b_hn_g=b_hn_g, c_ln_g=c_ln_g, c_ln_b=c_ln_b, c_w_s=c_w_s,
                  c_b_s=c_b_s, d_dw_w=d_dw_w, d_dw_b=d_dw_b, d_cn_g=d_cn_g, d_cn_b=d_cn_b, w_out=w_out,
                  norm2_g=norm2_g, p_w_q=p_w_q, p_sub_keys=p_sub_keys, p_u=p_u, p_v=p_v)
    bsz, seq, d = x.shape
    consts = _consts()
    rope = _rope_tables(positions.reshape(bsz * seq, 1), consts["invf"])
    xf = x.reshape(bsz * seq, d)
    tables = (p_u.astype(BF16), p_v.transpose(0, 2, 1).astype(BF16))
    for l in range(norm1_g.shape[0]):
        xf = _layer(xf, rope, consts, _prep_layer(l, params), tables, l, bsz, seq)
    return xf.reshape(bsz, seq, d)
```

```python
import functools

import jax
import jax.numpy as jnp
from jax import lax
from jax.experimental import pallas as pl
from jax.experimental.pallas import tpu as pltpu

F32 = jnp.float32
BF16 = jnp.bfloat16

D_MODEL = 1024
N_HEADS = 4
HEAD_DIM = 64
GROUP_W = N_HEADS * HEAD_DIM
Q_RANK, KV_RANK = 256, 128
NOPE, ROPE = 64, 32
QK_DIM = NOPE + ROPE
ROPE_BASE = 10000.0
LANE = 128
SUBLANES = 8
CONV_B = 4
CONV_D = 31
SP_CHUNK = 128
P_HEADS = 8
P_KEYS = 128
P_TOPK = 16
P_HALF = 64
N_EXPERTS = P_KEYS * P_KEYS
EPS = 1e-6
NEG = -1e30
VMEM_LIMIT = 56 * 1024 * 1024

NT = (((1,), (1,)), ((), ()))


def _cparams(sem, flags=None):
    return pltpu.CompilerParams(dimension_semantics=sem, vmem_limit_bytes=VMEM_LIMIT, flags=flags)


def _full(shape):
    n = len(shape)
    return pl.BlockSpec(shape, lambda *_: (0,) * n)


def _split3(a):
    hi = a.astype(BF16)
    r1 = a - hi.astype(F32)
    mid = r1.astype(BF16)
    lo = (r1 - mid.astype(F32)).astype(BF16)
    return hi, mid, lo


def _dot_exact_rhs(a, m_bf16):
    hi, mid, lo = _split3(a)
    d = lambda t: jnp.dot(t, m_bf16, preferred_element_type=F32)
    return d(hi) + d(mid) + d(lo)


def _dot_exact_lhs(m_bf16, a):
    hi, mid, lo = _split3(a)
    d = lambda t: jnp.dot(m_bf16, t, preferred_element_type=F32)
    return d(hi) + d(mid) + d(lo)


def _sigmoid(x):
    return 1.0 / (1.0 + jnp.exp(-x))


GELU_C1 = 0.7978845608028654
GELU_C2 = 0.044715


def _gelu(x, c1=None, c2=None):
    c = lambda v: jnp.asarray(v, x.dtype)
    c1 = c(GELU_C1) if c1 is None else c1
    c2 = c(GELU_C2) if c2 is None else c2
    inner = (c1 * x) * (c(1.0) + c2 * (x * x))
    return (c(0.5) * x) * (c(1.0) + jnp.tanh(inner))


PA_W, PB_W, PG_W, PC_W, PD_W = 512, 1024, 256, 512, 512
IN_COLS = (0, PA_W, PA_W + PB_W, PA_W + PB_W + PG_W, PA_W + PB_W + PG_W + PC_W,
           PA_W + PB_W + PG_W + PC_W + PD_W)


def _inproj_kernel(x_ref, g_ref, w_ref, pa_ref, pb_ref, pg_ref, pc_ref, pd_ref):
    x = x_ref[...]
    hn = (x * lax.rsqrt(jnp.mean(x * x, axis=-1, keepdims=True) + EPS) * g_ref[...]).astype(BF16)
    outs = (pa_ref, pb_ref, pg_ref, pc_ref, pd_ref)
    for k, o in enumerate(outs):
        y = jnp.dot(hn, w_ref[:, IN_COLS[k]:IN_COLS[k + 1]], preferred_element_type=F32)
        o[...] = y.astype(o.dtype)


def _inproj(x, g, w_all, tm=512):
    t = x.shape[0]
    widths = (PA_W, PB_W, PG_W, PC_W, PD_W)
    dts = (BF16, BF16, F32, BF16, BF16)
    return pl.pallas_call(
        _inproj_kernel,
        grid=(t // tm,),
        in_specs=[pl.BlockSpec((tm, D_MODEL), lambda i: (i, 0)),
                  _full((1, D_MODEL)), _full((D_MODEL, IN_COLS[-1]))],
        out_specs=[pl.BlockSpec((tm, w), lambda i: (i, 0)) for w in widths],
        out_shape=[jax.ShapeDtypeStruct((t, w), d) for w, d in zip(widths, dts)],
        compiler_params=_cparams(("parallel",)),
        name="inproj",
    )(x, g, w_all)


MLA_TQ = 256
MLA_PRO = 512


def _rope_kernel(pos_ref, invf_ref, cos_ref, sin_ref):
    ang = pos_ref[...].astype(F32) * invf_ref[...]
    cos_ref[...] = jnp.cos(ang)
    sin_ref[...] = jnp.sin(ang)


def _rope_tables(pos, invf, tm=1024):
    t = pos.shape[0]
    blk = pl.BlockSpec((tm, LANE), lambda i: (i, 0))
    return pl.pallas_call(
        _rope_kernel,
        grid=(t // tm,),
        in_specs=[pl.BlockSpec((tm, 1), lambda i: (i, 0)), _full((1, LANE))],
        out_specs=[blk, blk],
        out_shape=[jax.ShapeDtypeStruct((t, LANE), F32)] * 2,
        compiler_params=_cparams(("parallel",)),
        name="rope",
    )(pos, invf)


def _mla_kernel(pa_ref, cos_ref, sin_ref, cqg_ref, ckvg_ref, wuq_ref, wk_ref, wv_ref,
                qg_ref, kg_ref, o_ref, q_s, k_s, v_s, *, seq):
    tq = MLA_TQ

    def chunk(c, carry):
        r0 = pl.multiple_of(c * MLA_PRO, MLA_PRO)
        pa = pa_ref[0, pl.ds(r0, MLA_PRO), :]
        cq = pa[:, 0:Q_RANK].astype(F32)
        ckv = pa[:, Q_RANK:Q_RANK + KV_RANK].astype(F32)
        krp = pa[:, Q_RANK + KV_RANK:].astype(F32)
        cqn = (cq * lax.rsqrt(jnp.mean(cq * cq, -1, keepdims=True) + EPS) * cqg_ref[...]).astype(BF16)
        ckvn = (ckv * lax.rsqrt(jnp.mean(ckv * ckv, -1, keepdims=True) + EPS) * ckvg_ref[...]).astype(BF16)
        q = jnp.dot(cqn, wuq_ref[...], preferred_element_type=F32)
        kn = jnp.dot(ckvn, wk_ref[...], preferred_element_type=F32)
        v = jnp.dot(ckvn, wv_ref[...], preferred_element_type=F32)
        lane = lax.broadcasted_iota(jnp.int32, (MLA_PRO, LANE), 1)
        cosv = cos_ref[0, pl.ds(r0, MLA_PRO), :]
        sinv = sin_ref[0, pl.ds(r0, MLA_PRO), :]
        half = ROPE // 2
        sin_hi = jnp.where((lane >= NOPE + half) & (lane < QK_DIM), sinv, 0.0)
        sin_lo = jnp.where((lane >= NOPE) & (lane < NOPE + half), -sinv, 0.0)
        kr_sh = pltpu.roll(krp, NOPE, 1)

        def norm_rope(t, g):
            t = t * lax.rsqrt(jnp.sum(t * t, -1, keepdims=True) * (1.0 / QK_DIM) + EPS) * g
            return (t * cosv + pltpu.roll(t, half, 1) * sin_hi
                    + pltpu.roll(t, LANE - half, 1) * sin_lo)

        for h in range(N_HEADS):
            sl = slice(h * LANE, (h + 1) * LANE)
            q_s[pl.ds(r0, MLA_PRO), sl] = norm_rope(q[:, sl], qg_ref[...]).astype(BF16)
            k_s[pl.ds(r0, MLA_PRO), sl] = norm_rope(kn[:, sl] + kr_sh, kg_ref[...]).astype(BF16)
            v_s[pl.ds(r0, MLA_PRO), sl] = jnp.where(lane == HEAD_DIM, 1.0, v[:, sl]).astype(BF16)
        return carry

    lax.fori_loop(0, seq // MLA_PRO, chunk, 0)

    row = lax.broadcasted_iota(jnp.int32, (tq, tq), 0)
    col = lax.broadcasted_iota(jnp.int32, (tq, tq), 1)
    for i in range(seq // tq):
        klen = (i + 1) * tq
        for h in range(N_HEADS):
            sl = slice(h * LANE, (h + 1) * LANE)
            qh = q_s[i * tq:(i + 1) * tq, sl]
            s = lax.dot_general(qh, k_s[0:klen, sl], NT, preferred_element_type=F32)
            diag = jnp.where(col <= row, s[:, klen - tq:], NEG)
            s = diag if i == 0 else jnp.concatenate([s[:, :klen - tq], diag], axis=1)
            p = jnp.exp(s - jnp.max(s, -1, keepdims=True)).astype(BF16)
            acc = jnp.dot(p, v_s[0:klen, sl], preferred_element_type=F32)
            out_h = acc[:, :HEAD_DIM] / acc[:, HEAD_DIM:HEAD_DIM + 1]
            o_ref[0, i * tq:(i + 1) * tq, h * HEAD_DIM:(h + 1) * HEAD_DIM] = out_h.astype(BF16)


def _mla(pa, cosv, sinv, cqg, ckvg, wuq, wk, wv, qg, kg, bsz, seq):
    hw = N_HEADS * LANE
    seq_blk = lambda w: pl.BlockSpec((1, seq, w), lambda b: (b, 0, 0))
    return pl.pallas_call(
        functools.partial(_mla_kernel, seq=seq),
        grid=(bsz,),
        in_specs=[seq_blk(PA_W), seq_blk(LANE), seq_blk(LANE),
                  _full((1, Q_RANK)), _full((1, KV_RANK)),
                  _full((Q_RANK, hw)), _full((KV_RANK, hw)), _full((KV_RANK, hw)),
                  _full((1, LANE)), _full((1, LANE))],
        out_specs=seq_blk(GROUP_W),
        out_shape=jax.ShapeDtypeStruct((bsz, seq, GROUP_W), BF16),
        scratch_shapes=[pltpu.VMEM((seq, hw), BF16)] * 3,
        compiler_params=_cparams(("parallel",)),
        name="mla",
    )(pa, cosv, sinv, cqg, ckvg, wuq, wk, wv, qg, kg)


ML_L = 256
ML_HALO = 8
ML_NB = 2


def _mlstm_kernel(pb_ref, pg_ref, cw_ref, cb_ref, wq_ref, wk_ref, gb_ref, hng_ref, tri_ref,
                  o_ref, xpad, st, mst, fst):
    for bi in range(ML_NB):
        _mlstm_chunk(pb_ref.at[bi], pg_ref.at[bi], cw_ref, cb_ref, wq_ref, wk_ref, gb_ref, hng_ref, tri_ref,
                     o_ref.at[bi], xpad.at[bi], st.at[bi], mst.at[bi], fst.at[bi])


def _mlstm_chunk(pb_ref, pg_ref, cw_ref, cb_ref, wq_ref, wk_ref, gb_ref, hng_ref, tri_ref,
                 o_ref, xpad, st, mst, fst):
    c = pl.program_id(1)
    L = ML_L

    @pl.when(c == 0)
    def _init():
        xpad[0:ML_HALO, :] = jnp.zeros((ML_HALO, GROUP_W), F32)
        st[...] = jnp.zeros(st.shape, F32)
        mst[...] = jnp.full(mst.shape, NEG, F32)
        fst[...] = jnp.zeros(fst.shape, F32)

    xqk = pb_ref[:, 0:GROUP_W].astype(F32)
    xpad[ML_HALO:ML_HALO + L, :] = xqk
    acc = cb_ref[...] + jnp.zeros((L, GROUP_W), F32)
    for j in range(CONV_B):
        off = ML_HALO - (CONV_B - 1) + j
        acc = acc + cw_ref[j:j + 1, :] * xpad[off:off + L, :]
    xpad[0:ML_HALO, :] = xqk[L - ML_HALO:, :]
    xc = (acc * _sigmoid(acc)).astype(BF16)
    q = jnp.dot(xc, wq_ref[...], preferred_element_type=F32)
    k = jnp.dot(xc, wk_ref[...], preferred_element_type=F32)

    gates = pg_ref[...] + gb_ref[...]
    ig = gates[:, :LANE]
    fg = gates[:, LANE:]
    lf = jnp.minimum(fg, 0.0) - jnp.log(1.0 + jnp.exp(-jnp.abs(fg)))
    fcol = _dot_exact_lhs(tri_ref[...], lf) + fst[...]
    fst[...] = fcol[L - 1:L, :]
    a_col = ig - fcol
    a_t = a_col.T

    rowi = lax.broadcasted_iota(jnp.int32, (L, L), 0)
    coli = lax.broadcasted_iota(jnp.int32, (L, L), 1)
    causal = coli <= rowi
    lane_v = lax.broadcasted_iota(jnp.int32, (L, LANE), 1)
    lane_w = lax.broadcasted_iota(jnp.int32, (L, GROUP_W), 1)

    mts, mns, mcs = [], [], []
    for h in range(N_HEADS):
        amat = jnp.where(causal, a_t[h:h + 1, :], NEG)
        mc = mst[:, h:h + 1]
        mt = jnp.maximum(jnp.max(amat, -1, keepdims=True), mc)
        mts.append((amat, mt))
        mcs.append(mc)
        mns.append(mt[L - 1:L, :])

    wg = jnp.exp(a_col[:, N_HEADS - 1:N_HEADS] - mns[N_HEADS - 1])
    for h in range(N_HEADS - 2, -1, -1):
        wg = jnp.where(lane_w < (h + 1) * HEAD_DIM, jnp.exp(a_col[:, h:h + 1] - mns[h]), wg)
    kw_t = (k * wg).T.astype(BF16)

    for h in range(N_HEADS):
        hs = slice(h * HEAD_DIM, (h + 1) * HEAD_DIM)
        amat, mt = mts[h]
        mc, mn = mcs[h], mns[h]
        p = jnp.exp(amat - mt)
        qh = q[:, hs].astype(BF16)
        kh = k[:, hs].astype(BF16)
        qk = lax.dot_general(qh, kh, NT, preferred_element_type=F32)
        w = (p * qk).astype(BF16)
        vext = jnp.where(lane_v == HEAD_DIM, 1.0,
                         pb_ref[:, GROUP_W + h * LANE:GROUP_W + (h + 1) * LANE].astype(F32)).astype(BF16)
        sth = st[h]
        nd = (jnp.dot(w, vext, preferred_element_type=F32)
              + jnp.exp(mc - mt) * jnp.dot(qh, sth.astype(BF16), preferred_element_type=F32))
        den = nd[:, HEAD_DIM:HEAD_DIM + 1]
        floor = jnp.exp(-(fcol[:, h:h + 1] + mt))
        hh = nd[:, :HEAD_DIM] / jnp.maximum(jnp.abs(den), floor)
        hh = hh * lax.rsqrt(jnp.mean(hh * hh, -1, keepdims=True) + EPS) * hng_ref[:, hs]
        xo = pb_ref[:, GROUP_W + N_HEADS * LANE + h * HEAD_DIM:
                    GROUP_W + N_HEADS * LANE + (h + 1) * HEAD_DIM].astype(F32)
        o_ref[:, hs] = (_sigmoid(xo) * hh).astype(BF16)
        st[h] = jnp.exp(mc - mn) * sth + jnp.dot(kw_t[hs, :], vext, preferred_element_type=F32)
        mst[:, h:h + 1] = mn


def _mlstm(pb, pg, cw, cb, wq, wk, gb, hng, tri, bsz, seq):
    L = ML_L
    return pl.pallas_call(
        _mlstm_kernel,
        grid=(bsz // ML_NB, seq // L),
        in_specs=[pl.BlockSpec((ML_NB, L, PB_W), lambda b, c: (b, c, 0)),
                  pl.BlockSpec((ML_NB, L, PG_W), lambda b, c: (b, c, 0)),
                  _full((CONV_B, GROUP_W)), _full((1, GROUP_W)),
                  _full((GROUP_W, GROUP_W)), _full((GROUP_W, GROUP_W)),
                  _full((1, PG_W)), _full((1, GROUP_W)), _full((L, L))],
        out_specs=pl.BlockSpec((ML_NB, L, GROUP_W), lambda b, c: (b, c, 0)),
        out_shape=jax.ShapeDtypeStruct((bsz, seq, GROUP_W), BF16),
        scratch_shapes=[pltpu.VMEM((ML_NB, L + ML_HALO, GROUP_W), F32),
                        pltpu.VMEM((ML_NB, N_HEADS, HEAD_DIM, LANE), F32),
                        pltpu.VMEM((ML_NB, 1, LANE), F32),
                        pltpu.VMEM((ML_NB, 1, LANE), F32)],
        compiler_params=_cparams(("parallel", "arbitrary")),
        name="mlstm",
    )(pb, pg, cw, cb, wq, wk, gb, hng, tri)


SP_STEP = 4


def _spatial_kernel(pc_ref, g_ref, b_ref, ws_ref, bias_ref, o_ref):
    T = SP_CHUNK
    rowi = lax.broadcasted_iota(jnp.int32, (T, N_HEADS * T), 0)
    coli = lax.broadcasted_iota(jnp.int32, (T, N_HEADS * T), 1)
    ws = jnp.where((coli & (T - 1)) <= rowi, ws_ref[...], 0.0).astype(BF16)
    rgrp = lax.broadcasted_iota(jnp.int32, (T, GROUP_W), 1) // HEAD_DIM
    for cidx in range(SP_STEP):
        rs = slice(cidx * T, (cidx + 1) * T)
        u = _gelu(pc_ref[rs, 0:GROUP_W].astype(F32))
        gv = _gelu(pc_ref[rs, GROUP_W:].astype(F32))
        mu = jnp.mean(gv, -1, keepdims=True)
        dv = gv - mu
        var = jnp.mean(dv * dv, -1, keepdims=True)
        vn = (dv * lax.rsqrt(var + EPS) * g_ref[...] + b_ref[...]).astype(BF16)
        zero = jnp.zeros_like(vn)
        vbig = jnp.concatenate([jnp.where(rgrp == g, vn, zero) for g in range(N_HEADS)], axis=0)
        sg = jnp.dot(ws, vbig, preferred_element_type=F32) + bias_ref[...]
        o_ref[rs, :] = (u * sg).astype(BF16)


def _spatial(pc, g, b, ws_cat, bias_full):
    t = pc.shape[0]
    tm = SP_STEP * SP_CHUNK
    return pl.pallas_call(
        _spatial_kernel,
        grid=(t // tm,),
        in_specs=[pl.BlockSpec((tm, PC_W), lambda i: (i, 0)),
                  _full((1, GROUP_W)), _full((1, GROUP_W)),
                  _full((SP_CHUNK, N_HEADS * SP_CHUNK)), _full((SP_CHUNK, GROUP_W))],
        out_specs=pl.BlockSpec((tm, GROUP_W), lambda i: (i, 0)),
        out_shape=jax.ShapeDtypeStruct((t, GROUP_W), BF16),
        compiler_params=_cparams(("parallel",)),
        name="spatial",
    )(pc, g, b, ws_cat, bias_full)


CV_T = 256
CV_HALO = 32


def _convmod_kernel(pd_ref, w_ref, b_ref, avg_ref, g_ref, be_ref, o_ref, ypad, zsh):
    i = pl.program_id(1)
    T = CV_T

    @pl.when(i == 0)
    def _init():
        ypad[0:CV_HALO, :] = jnp.zeros((CV_HALO, GROUP_W), F32)

    a = pd_ref[0, :, 0:GROUP_W].astype(F32)
    b = pd_ref[0, :, GROUP_W:].astype(F32)
    y = a * _sigmoid(b)
    ypad[CV_HALO:CV_HALO + T, :] = y
    acc = b_ref[...] + jnp.zeros((T, GROUP_W), F32)
    off0 = CV_HALO - (CONV_D - 1)
    for b in range(SUBLANES):
        offs = [o for o in range(off0, off0 + CONV_D) if o % SUBLANES == b]
        span = offs[-1] - b + T
        src = ypad
        if b:
            zsh[0:span, :] = ypad[b:b + span, :]
            src = zsh
        for o in offs:
            acc = acc + w_ref[o - off0:o - off0 + 1, :] * src[o - b:o - b + T, :]
    ypad[0:CV_HALO, :] = y[T - CV_HALO:, :]
    mu = _dot_exact_rhs(acc, avg_ref[...])
    dv = acc - mu
    var = _dot_exact_rhs(dv * dv, avg_ref[...])
    yn = dv * lax.rsqrt(var + EPS) * g_ref[...] + be_ref[...]
    o_ref[0] = (yn * _sigmoid(yn)).astype(BF16)


def _convmod(pd, w, b, avg, g, be, bsz, seq):
    return pl.pallas_call(
        _convmod_kernel,
        grid=(bsz, seq // CV_T),
        in_specs=[pl.BlockSpec((1, CV_T, PD_W), lambda bb, i: (bb, i, 0)),
                  _full((CONV_D, GROUP_W)), _full((1, GROUP_W)), _full((GROUP_W, GROUP_W)),
                  _full((1, GROUP_W)), _full((1, GROUP_W))],
        out_specs=pl.BlockSpec((1, CV_T, GROUP_W), lambda bb, i: (bb, i, 0)),
        out_shape=jax.ShapeDtypeStruct((bsz, seq, GROUP_W), BF16),
        scratch_shapes=[pltpu.VMEM((CV_T + CV_HALO, GROUP_W), F32)] * 2,
        compiler_params=_cparams(("parallel", "arbitrary")),
        name="convmod",
    )(pd, w, b, avg, g, be)


OP_TM = 512
PQ_W = P_HEADS * P_HALF
KROWS = P_KEYS * P_HEADS


def _outproj_kernel(x_ref, ma_ref, mb_ref, mc_ref, md_ref, wo_ref, g_ref, wpq_ref, k1_ref, k2_ref,
                    h_ref, xnt_ref, s1_ref, s2_ref):
    acc = x_ref[...]
    for g, m in enumerate((ma_ref, mb_ref, mc_ref, md_ref)):
        acc = acc + jnp.dot(m[...], wo_ref[g * GROUP_W:(g + 1) * GROUP_W, :], preferred_element_type=F32)
    h_ref[...] = acc
    hn = acc * lax.rsqrt(jnp.mean(acc * acc, -1, keepdims=True) + EPS) * g_ref[...]
    xnt_ref[...] = hn.T.astype(BF16)
    q = jnp.dot(hn.astype(BF16), wpq_ref[...], preferred_element_type=F32).astype(BF16)
    s1_ref[...] = lax.dot_general(k1_ref[...], q[:, :PQ_W], NT, preferred_element_type=F32)
    s2_ref[...] = lax.dot_general(k2_ref[...], q[:, PQ_W:], NT, preferred_element_type=F32)


def _outproj(x, ma, mb, mc, md, wo, g, wpq, k1, k2):
    t = x.shape[0]
    tm = OP_TM
    mix = pl.BlockSpec((tm, GROUP_W), lambda i: (i, 0))
    tcol = pl.BlockSpec((KROWS, tm), lambda i: (0, i))
    return pl.pallas_call(
        _outproj_kernel,
        grid=(t // tm,),
        in_specs=[pl.BlockSpec((tm, D_MODEL), lambda i: (i, 0)), mix, mix, mix, mix,
                  _full((D_MODEL, D_MODEL)), _full((1, D_MODEL)), _full((D_MODEL, 2 * PQ_W)),
                  _full((KROWS, PQ_W)), _full((KROWS, PQ_W))],
        out_specs=[pl.BlockSpec((tm, D_MODEL), lambda i: (i, 0)),
                   pl.BlockSpec((D_MODEL, tm), lambda i: (0, i)), tcol, tcol],
        out_shape=[jax.ShapeDtypeStruct((t, D_MODEL), F32),
                   jax.ShapeDtypeStruct((D_MODEL, t), BF16),
                   jax.ShapeDtypeStruct((KROWS, t), F32),
                   jax.ShapeDtypeStruct((KROWS, t), F32)],
        compiler_params=_cparams(("parallel",)),
        name="outproj",
    )(x, ma, mb, mc, md, wo, g, wpq, k1, k2)


RT_T = LANE
BF_ROWS = 16


def _bitonic_merge(v):
    n = len(v)
    if n == 1:
        return v
    half = n // 2
    hi = [jnp.maximum(v[i], v[i + half]) for i in range(half)]
    lo = [jnp.minimum(v[i], v[i + half]) for i in range(half)]
    return _bitonic_merge(hi) + _bitonic_merge(lo)


def _bitonic_sort(v):
    n = len(v)
    if n == 1:
        return v
    return _bitonic_merge(_bitonic_sort(v[:n // 2]) + _bitonic_sort(v[n // 2:])[::-1])


def _merge_top(a, b):
    n = len(a)
    c = list(a)
    for k, bv in enumerate(b):
        c[n - 1 - k] = jnp.maximum(a[n - 1 - k], bv)
    return _bitonic_merge(c)


def _sorted_top(ref, lanes):
    top = None
    for grp in range(P_KEYS // P_TOPK):
        tiles = [ref[(grp * P_TOPK + k) * P_HEADS:(grp * P_TOPK + k + 1) * P_HEADS, lanes]
                 for k in range(P_TOPK)]
        srt = _bitonic_sort(tiles)
        top = srt if top is None else _merge_top(top, srt)
    return top


def _route_kernel(s1_ref, s2_ref, cnt_ref, w_ref, rk_ref, p2h_ref):
    for g in range(RT_T // LANE):
        lanes = slice(g * LANE, (g + 1) * LANE)
        t1 = _sorted_top(s1_ref, lanes)
        t2 = _sorted_top(s2_ref, lanes)
        rows = [[t1[a] + t2[b] for b in range(P_TOPK // (a + 1))] for a in range(P_TOPK)]
        top = rows[0]
        for a in range(1, P_TOPK):
            top = _merge_top(top, rows[a])
        tau = top[P_TOPK - 1]
        cmax = rows[0][0]
        z = jnp.zeros((P_HEADS, LANE), F32)
        cnt_a = []
        for a in range(P_TOPK):
            ca = jnp.zeros((P_HEADS, LANE), F32)
            for cnd in rows[a]:
                sel = cnd >= tau
                z = z + jnp.where(sel, jnp.exp(cnd - cmax), 0.0)
                ca = ca + jnp.where(sel, 1.0, 0.0)
            cnt_a.append(ca)
        zinv = 1.0 / z

        def body(n, carry):
            r0 = pl.multiple_of(n * P_HEADS, P_HEADS)
            s1 = s1_ref[pl.ds(r0, P_HEADS), lanes]
            cn = jnp.where(s1 + t2[0] >= tau, 1.0, 0.0)
            for a in range(P_TOPK // 2 - 1, -1, -1):
                cn = jnp.where(s1 == t1[a], cnt_a[a], cn)
            cnt_ref[pl.ds(r0, P_HEADS), lanes] = cn
            w_ref[pl.ds(r0, P_HEADS), lanes] = jnp.exp(s1 - t1[0]) * zinv
            return carry

        lax.fori_loop(0, P_KEYS, body, 0, unroll=4)

        for h in range(P_HEADS):
            tb = [jnp.broadcast_to(t2[b][h:h + 1, :], (BF_ROWS, LANE)) for b in range(P_TOPK)]

            def rbody(r, carry, h=h, tb=tb):
                r0 = pl.multiple_of(h * P_KEYS + r * BF_ROWS, BF_ROWS)
                s2 = s2_ref[pl.ds(h + r * (BF_ROWS * P_HEADS), BF_ROWS, stride=P_HEADS), lanes]
                rk = jnp.full((BF_ROWS, LANE), float(P_TOPK), F32)
                for b in range(P_TOPK - 1, -1, -1):
                    rk = jnp.where(tb[b] > s2, rk, float(b))
                rk_ref[pl.ds(r0, BF_ROWS), lanes] = rk.astype(BF16)
                p2h_ref[pl.ds(r0, BF_ROWS), lanes] = jnp.exp(s2 - tb[0]).astype(BF16)
                return carry

            lax.fori_loop(0, P_KEYS // BF_ROWS, rbody, 0, unroll=2)


def _route(s1, s2):
    t = s1.shape[1]
    blk = pl.BlockSpec((KROWS, RT_T), lambda i: (0, i))
    return pl.pallas_call(
        _route_kernel,
        grid=(t // RT_T,),
        in_specs=[blk, blk],
        out_specs=[blk, blk, blk, blk],
        out_shape=[jax.ShapeDtypeStruct((KROWS, t), F32), jax.ShapeDtypeStruct((KROWS, t), F32),
                   jax.ShapeDtypeStruct((KROWS, t), BF16), jax.ShapeDtypeStruct((KROWS, t), BF16)],
        compiler_params=_cparams(("parallel",)),
        name="route",
    )(s1, s2)


PE_TT = 512
PE_EB = 1024
PE_K1 = PE_EB // P_KEYS


PE_GL = 256


def _peer_kernel(xnt_ref, u_ref, vtp_ref, vtl_ref, cnt_ref, w_ref, rk_ref, p2h_ref, h_ref, gc_ref, o_ref,
                 acc, ht, gsc, cnt_s, w_s, rk_s, p2_s):
    j = pl.program_id(1)
    slot = j % 2

    @pl.when(j == 0)
    def _init():
        acc[...] = jnp.zeros(acc.shape, F32)
        ht[1] = jnp.zeros(ht.shape[1:], BF16)
        rk_s[...] = rk_ref[...]
        p2_s[...] = p2h_ref[...]

    e0 = j * PE_K1
    cnt_s[...] = cnt_ref[pl.ds(e0, PE_K1)]
    w_s[...] = w_ref[pl.ds(e0, PE_K1)]

    ngrp = P_KEYS // BF_ROWS
    bzero = jnp.zeros((), BF16)
    gtile = lambda r: jnp.concatenate(
        [jnp.broadcast_to(gc_ref[r:r + 1, :], (BF_ROWS, PE_GL)).astype(BF16)] * (PE_EB // BF_ROWS), axis=0)
    gc1, gc2 = gtile(0), gtile(1)
    for s in range(PE_TT // PE_GL):
        ls = slice(s * PE_GL, (s + 1) * PE_GL)
        acc[:, ls] += jnp.dot(vtp_ref[...], ht[1 - slot, :, ls], preferred_element_type=F32)
        for c in range(PE_K1):
            gate = [jnp.zeros((BF_ROWS, PE_GL), BF16) for _ in range(ngrp)]
            for h in range(P_HEADS):
                cn = jnp.broadcast_to(cnt_s[c, h:h + 1, ls], (BF_ROWS, PE_GL)).astype(BF16)
                wg = jnp.broadcast_to(w_s[c, h:h + 1, ls], (BF_ROWS, PE_GL)).astype(BF16)
                for r in range(ngrp):
                    rows = slice(h * P_KEYS + r * BF_ROWS, h * P_KEYS + (r + 1) * BF_ROWS)
                    sel = jnp.where(rk_s[rows, ls] < cn, p2_s[rows, ls], bzero)
                    gate[r] = gate[r] + sel * wg
            for r in range(ngrp):
                rows = slice(c * P_KEYS + r * BF_ROWS, c * P_KEYS + (r + 1) * BF_ROWS)
                gsc[rows, ls] = gate[r]
        a = jnp.dot(u_ref[...], xnt_ref[:, ls], preferred_element_type=F32)
        ht[slot, :, ls] = _gelu(a.astype(BF16), gc1, gc2) * gsc[:, ls]

    @pl.when(j == pl.num_programs(1) - 1)
    def _fin():
        last = jnp.dot(vtl_ref[...], ht[slot], preferred_element_type=F32)
        o_ref[...] = h_ref[...] + (acc[...] + last).T


def _peer(xnt, u_all, vt_all, layer, cnt, w, rk, p2h, hres):
    t = xnt.shape[1]
    nj = N_EXPERTS // PE_EB
    gconst = jnp.zeros((SUBLANES, PE_GL), F32).at[0].set(GELU_C1).at[1].set(GELU_C2)
    tok = lambda r: pl.BlockSpec((r, PE_TT), lambda i, j: (0, i))
    tok3 = pl.BlockSpec((P_KEYS, P_HEADS, PE_TT), lambda i, j: (0, 0, i))
    return pl.pallas_call(
        _peer_kernel,
        grid=(t // PE_TT, nj),
        in_specs=[tok(D_MODEL),
                  pl.BlockSpec((None, PE_EB, D_MODEL), lambda i, j: (layer, j, 0)),
                  pl.BlockSpec((None, D_MODEL, PE_EB), lambda i, j: (layer, 0, jnp.maximum(j - 1, 0))),
                  pl.BlockSpec((None, D_MODEL, PE_EB), lambda i, j: (layer, 0, nj - 1)),
                  tok3, tok3, tok(KROWS), tok(KROWS),
                  pl.BlockSpec((PE_TT, D_MODEL), lambda i, j: (i, 0)),
                  _full((SUBLANES, PE_GL))],
        out_specs=pl.BlockSpec((PE_TT, D_MODEL), lambda i, j: (i, 0)),
        out_shape=jax.ShapeDtypeStruct((t, D_MODEL), F32),
        scratch_shapes=[pltpu.VMEM((D_MODEL, PE_TT), F32), pltpu.VMEM((2, PE_EB, PE_TT), BF16),
                        pltpu.VMEM((PE_EB, PE_TT), BF16),
                        pltpu.VMEM((PE_K1, P_HEADS, PE_TT), F32), pltpu.VMEM((PE_K1, P_HEADS, PE_TT), F32),
                        pltpu.VMEM((KROWS, PE_TT), BF16), pltpu.VMEM((KROWS, PE_TT), BF16)],
        compiler_params=_cparams(("parallel", "arbitrary")),
        name="peer",
    )(xnt, u_all, vt_all, vt_all, cnt, w, rk, p2h, hres, gconst)


def _pad_cols(w, width):
    return jnp.pad(w, ((0, 0), (0, width - w.shape[1])))


def _prep_layer(l, p):
    w_in = p["w_in"][l]
    o = 0
    cuts = {}
    for name, wdt in (("a", Q_RANK + KV_RANK + ROPE), ("bqk", GROUP_W), ("bv", GROUP_W), ("bo", GROUP_W),
                      ("bi", N_HEADS), ("bf", N_HEADS), ("c", 2 * GROUP_W), ("d", 2 * GROUP_W)):
        cuts[name] = w_in[:, o:o + wdt]
        o += wdt
    bv = cuts["bv"].reshape(D_MODEL, N_HEADS, HEAD_DIM)
    bv = jnp.pad(bv, ((0, 0), (0, 0), (0, LANE - HEAD_DIM))).reshape(D_MODEL, N_HEADS * LANE)
    w_all = jnp.concatenate([
        _pad_cols(cuts["a"], PA_W), cuts["bqk"], bv, cuts["bo"],
        _pad_cols(cuts["bi"], LANE), _pad_cols(cuts["bf"], LANE), cuts["c"], cuts["d"]], axis=1).astype(BF16)

    def head_pad(w, width):
        return jnp.pad(w, ((0, 0), (0, 0), (0, LANE - width))).reshape(w.shape[0], N_HEADS * LANE)

    wuq = head_pad(p["a_w_uq"][l].reshape(Q_RANK, N_HEADS, QK_DIM), QK_DIM).astype(BF16)
    wukv = p["a_w_ukv"][l].reshape(KV_RANK, N_HEADS, NOPE + HEAD_DIM)
    wk = head_pad(wukv[:, :, :NOPE], NOPE).astype(BF16)
    wv = head_pad(wukv[:, :, NOPE:], HEAD_DIM).astype(BF16)
    qg = _pad_cols(p["a_qn_g"][l][None, :] * (QK_DIM ** -0.5), LANE)
    kg = _pad_cols(p["a_kn_g"][l][None, :], LANE)

    eye = jnp.eye(N_HEADS, dtype=F32)
    wq_bd = jnp.einsum("hde,hg->hdge", p["b_w_q"][l], eye).reshape(GROUP_W, GROUP_W).astype(BF16)
    wk_bd = (jnp.einsum("hde,hg->hdge", p["b_w_k"][l], eye).reshape(GROUP_W, GROUP_W)
             * (HEAD_DIM ** -0.5)).astype(BF16)
    gb = jnp.concatenate([_pad_cols(p["b_b_i"][l][None, :], LANE), _pad_cols(p["b_b_f"][l][None, :], LANE)], axis=1)

    ws_cat = p["c_w_s"][l].transpose(1, 0, 2).reshape(SP_CHUNK, N_HEADS * SP_CHUNK)
    bias_full = jnp.repeat(p["c_b_s"][l].T, HEAD_DIM, axis=1)

    wpq = p["p_w_q"][l].reshape(D_MODEL, P_HEADS, 2, P_HALF).transpose(0, 2, 1, 3).reshape(D_MODEL, 2 * PQ_W)
    keys = p["p_sub_keys"][l]
    eye8 = jnp.eye(P_HEADS, dtype=F32)
    kmat = [jnp.einsum("hnd,hg->nhgd", keys[:, s], eye8).reshape(KROWS, PQ_W).astype(BF16) for s in range(2)]
    return dict(
        norm1_g=p["norm1_g"][l][None, :], w_all=w_all,
        cqg=p["a_cq_g"][l][None, :], ckvg=p["a_ckv_g"][l][None, :], wuq=wuq, wk=wk, wv=wv, qg=qg, kg=kg,
        b_cw=p["b_conv_w"][l], b_cb=p["b_conv_b"][l][None, :], wq_bd=wq_bd, wk_bd=wk_bd, gb=gb,
        hng=p["b_hn_g"][l].reshape(1, GROUP_W),
        c_g=p["c_ln_g"][l][None, :], c_b=p["c_ln_b"][l][None, :], ws_cat=ws_cat, bias_full=bias_full,
        d_w=p["d_dw_w"][l], d_b=p["d_dw_b"][l][None, :],
        d_g=p["d_cn_g"][l].reshape(1, GROUP_W), d_be=p["d_cn_b"][l].reshape(1, GROUP_W),
        w_out=p["w_out"][l].astype(BF16), norm2_g=p["norm2_g"][l][None, :], wpq=wpq.astype(BF16),
        k1=kmat[0], k2=kmat[1],
    )


def _layer(x, rope, consts, w, tables, layer, bsz, seq):
    t = bsz * seq
    pa, pb, pg, pc, pd = _inproj(x, w["norm1_g"], w["w_all"])
    r3 = lambda a: a.reshape(bsz, seq, a.shape[-1])
    mix_a = _mla(r3(pa), r3(rope[0]), r3(rope[1]), w["cqg"], w["ckvg"], w["wuq"], w["wk"], w["wv"],
                 w["qg"], w["kg"], bsz, seq).reshape(t, GROUP_W)
    mix_b = _mlstm(r3(pb), r3(pg), w["b_cw"], w["b_cb"], w["wq_bd"], w["wk_bd"], w["gb"], w["hng"],
                   consts["tri"], bsz, seq).reshape(t, GROUP_W)
    mix_c = _spatial(pc, w["c_g"], w["c_b"], w["ws_cat"], w["bias_full"])
    mix_d = _convmod(r3(pd), w["d_w"], w["d_b"], consts["avg"], w["d_g"], w["d_be"], bsz, seq).reshape(t, GROUP_W)
    h, xnt, s1, s2 = _outproj(x, mix_a, mix_b, mix_c, mix_d, w["w_out"], w["norm2_g"], w["wpq"],
                              w["k1"], w["k2"])
    cnt, wgt, rk, p2h = _route(s1, s2)
    shp = (P_KEYS, P_HEADS, t)
    return _peer(xnt, tables[0], tables[1], layer, cnt.reshape(shp), wgt.reshape(shp), rk, p2h, h)


def _consts():
    half = ROPE // 2
    inv_freq = ROPE_BASE ** (-jnp.arange(0, ROPE, 2, dtype=F32) / ROPE)
    invf = jnp.zeros((1, LANE), F32)
    invf = invf.at[0, NOPE:NOPE + half].set(inv_freq).at[0, NOPE + half:QK_DIM].set(inv_freq)
    tri = jnp.tril(jnp.ones((ML_L, ML_L), F32)).astype(BF16)
    grp = jnp.arange(GROUP_W) // HEAD_DIM
    avg = ((grp[:, None] == grp[None, :]).astype(F32) / HEAD_DIM).astype(BF16)
    return dict(invf=invf, tri=tri, avg=avg)


def kernel(x, positions, norm1_g, w_in, a_cq_g, a_ckv_g, a_w_uq, a_w_ukv, a_qn_g, a_kn_g, b_conv_w, b_conv_b, b_w_q, b_w_k, b_b_i, b_b_f, b_hn_g, c_ln_g, c_ln_b, c_w_s, c_b_s, d_dw_w, d_dw_b, d_cn_g, d_cn_b, w_out, norm2_g, p_w_q, p_sub_keys, p_u, p_v):
    params = dict(norm1_g=norm1_g, w_in=w_in, a_cq_g=a_cq_g, a_ckv_g=a_ckv_g, a_w_uq=a_w_uq, a_w_ukv=a_w_ukv,
                  a_qn_g=a_qn_g, a_kn_g=a_kn_g, b_conv_w=b_conv_w, b_conv_b=b_conv_b, b_w_q=b_w_q, b_w_k=b_w_k,
                  b_b_i=b_b_i, b_b_f=b_b_f, b_hn_g=b_hn_g, c_ln_g=c_ln_g, c_ln_b=c_ln_b, c_w_s=c_w_s,
                  c_b_s=c_b_s, d_dw_w=d_dw_w, d_dw_b=d_dw_b, d_cn_g=d_cn_g, d_cn_b=d_cn_b, w_out=w_out,
                  norm2_g=norm2_g, p_w_q=p_w_q, p_sub_keys=p_sub_keys, p_u=p_u, p_v=p_v)
    bsz, seq, d = x.shape
    consts = _consts()
    rope = _rope_tables(positions.reshape(bsz * seq, 1), consts["invf"])
    xf = x.reshape(bsz * seq, d)
    tables = (p_u.astype(BF16), p_v.astype(BF16).transpose(0, 2, 1))
    for l in range(norm1_g.shape[0]):
        xf = _layer(xf, rope, consts, _prep_layer(l, params), tables, l, bsz, seq)
    return xf.reshape(bsz, seq, d)
```

```python
import functools

import jax
import jax.numpy as jnp
from jax import lax
from jax.experimental import pallas as pl
from jax.experimental.pallas import tpu as pltpu

F32 = jnp.float32
BF16 = jnp.bfloat16

D_MODEL = 1024
N_HEADS = 4
HEAD_DIM = 64
GROUP_W = N_HEADS * HEAD_DIM
Q_RANK, KV_RANK = 256, 128
NOPE, ROPE = 64, 32
QK_DIM = NOPE + ROPE
ROPE_BASE = 10000.0
LANE = 128
SUBLANES = 8
CONV_B = 4
CONV_D = 31
SP_CHUNK = 128
P_HEADS = 8
P_KEYS = 128
P_TOPK = 16
P_HALF = 64
N_EXPERTS = P_KEYS * P_KEYS
EPS = 1e-6
NEG = -1e30
VMEM_LIMIT = 56 * 1024 * 1024

NT = (((1,), (1,)), ((), ()))


def _cparams(sem, flags=None):
    return pltpu.CompilerParams(dimension_semantics=sem, vmem_limit_bytes=VMEM_LIMIT, flags=flags)


def _full(shape):
    n = len(shape)
    return pl.BlockSpec(shape, lambda *_: (0,) * n)


def _split3(a):
    hi = a.astype(BF16)
    r1 = a - hi.astype(F32)
    mid = r1.astype(BF16)
    lo = (r1 - mid.astype(F32)).astype(BF16)
    return hi, mid, lo


def _dot_exact_rhs(a, m_bf16):
    hi, mid, lo = _split3(a)
    d = lambda t: jnp.dot(t, m_bf16, preferred_element_type=F32)
    return d(hi) + d(mid) + d(lo)


def _dot_exact_lhs(m_bf16, a):
    hi, mid, lo = _split3(a)
    d = lambda t: jnp.dot(m_bf16, t, preferred_element_type=F32)
    return d(hi) + d(mid) + d(lo)


def _sigmoid(x):
    return 1.0 / (1.0 + jnp.exp(-x))


GELU_C1 = 0.7978845608028654
GELU_C2 = 0.044715


def _gelu(x, c1=None, c2=None):
    c = lambda v: jnp.asarray(v, x.dtype)
    c1 = c(GELU_C1) if c1 is None else c1
    c2 = c(GELU_C2) if c2 is None else c2
    inner = (c1 * x) * (c(1.0) + c2 * (x * x))
    return (c(0.5) * x) * (c(1.0) + jnp.tanh(inner))


PA_W, PB_W, PG_W, PC_W, PD_W = 512, 1024, 256, 512, 512
IN_COLS = (0, PA_W, PA_W + PB_W, PA_W + PB_W + PG_W, PA_W + PB_W + PG_W + PC_W,
           PA_W + PB_W + PG_W + PC_W + PD_W)


def _inproj_kernel(x_ref, g_ref, w_ref, pa_ref, pb_ref, pg_ref, pc_ref, pd_ref):
    x = x_ref[...]
    hn = (x * lax.rsqrt(jnp.mean(x * x, axis=-1, keepdims=True) + EPS) * g_ref[...]).astype(BF16)
    outs = (pa_ref, pb_ref, pg_ref, pc_ref, pd_ref)
    for k, o in enumerate(outs):
        y = jnp.dot(hn, w_ref[:, IN_COLS[k]:IN_COLS[k + 1]], preferred_element_type=F32)
        o[...] = y.astype(o.dtype)


def _inproj(x, g, w_all, tm=512):
    t = x.shape[0]
    widths = (PA_W, PB_W, PG_W, PC_W, PD_W)
    dts = (BF16, BF16, F32, BF16, BF16)
    return pl.pallas_call(
        _inproj_kernel,
        grid=(t // tm,),
        in_specs=[pl.BlockSpec((tm, D_MODEL), lambda i: (i, 0)),
                  _full((1, D_MODEL)), _full((D_MODEL, IN_COLS[-1]))],
        out_specs=[pl.BlockSpec((tm, w), lambda i: (i, 0)) for w in widths],
        out_shape=[jax.ShapeDtypeStruct((t, w), d) for w, d in zip(widths, dts)],
        compiler_params=_cparams(("parallel",)),
        name="inproj",
    )(x, g, w_all)


MLA_TQ = 256
MLA_PRO = 512


def _rope_kernel(pos_ref, invf_ref, cos_ref, sin_ref):
    ang = pos_ref[...].astype(F32) * invf_ref[...]
    cos_ref[...] = jnp.cos(ang)
    sin_ref[...] = jnp.sin(ang)


def _rope_tables(pos, invf, tm=1024):
    t = pos.shape[0]
    blk = pl.BlockSpec((tm, LANE), lambda i: (i, 0))
    return pl.pallas_call(
        _rope_kernel,
        grid=(t // tm,),
        in_specs=[pl.BlockSpec((tm, 1), lambda i: (i, 0)), _full((1, LANE))],
        out_specs=[blk, blk],
        out_shape=[jax.ShapeDtypeStruct((t, LANE), F32)] * 2,
        compiler_params=_cparams(("parallel",)),
        name="rope",
    )(pos, invf)


def _mla_kernel(pa_ref, cos_ref, sin_ref, cqg_ref, ckvg_ref, wuq_ref, wk_ref, wv_ref,
                qg_ref, kg_ref, o_ref, q_s, k_s, v_s, *, seq):
    tq = MLA_TQ

    def chunk(c, carry):
        r0 = pl.multiple_of(c * MLA_PRO, MLA_PRO)
        pa = pa_ref[0, pl.ds(r0, MLA_PRO), :]
        cq = pa[:, 0:Q_RANK].astype(F32)
        ckv = pa[:, Q_RANK:Q_RANK + KV_RANK].astype(F32)
        krp = pa[:, Q_RANK + KV_RANK:].astype(F32)
        cqn = (cq * lax.rsqrt(jnp.mean(cq * cq, -1, keepdims=True) + EPS) * cqg_ref[...]).astype(BF16)
        ckvn = (ckv * lax.rsqrt(jnp.mean(ckv * ckv, -1, keepdims=True) + EPS) * ckvg_ref[...]).astype(BF16)
        q = jnp.dot(cqn, wuq_ref[...], preferred_element_type=F32)
        kn = jnp.dot(ckvn, wk_ref[...], preferred_element_type=F32)
        v = jnp.dot(ckvn, wv_ref[...], preferred_element_type=F32)
        lane = lax.broadcasted_iota(jnp.int32, (MLA_PRO, LANE), 1)
        cosv = cos_ref[0, pl.ds(r0, MLA_PRO), :]
        sinv = sin_ref[0, pl.ds(r0, MLA_PRO), :]
        half = ROPE // 2
        sin_hi = jnp.where((lane >= NOPE + half) & (lane < QK_DIM), sinv, 0.0)
        sin_lo = jnp.where((lane >= NOPE) & (lane < NOPE + half), -sinv, 0.0)
        kr_sh = pltpu.roll(krp, NOPE, 1)

        def norm_rope(t, g):
            t = t * lax.rsqrt(jnp.sum(t * t, -1, keepdims=True) * (1.0 / QK_DIM) + EPS) * g
            return (t * cosv + pltpu.roll(t, half, 1) * sin_hi
                    + pltpu.roll(t, LANE - half, 1) * sin_lo)

        for h in range(N_HEADS):
            sl = slice(h * LANE, (h + 1) * LANE)
            q_s[pl.ds(r0, MLA_PRO), sl] = norm_rope(q[:, sl], qg_ref[...]).astype(BF16)
            k_s[pl.ds(r0, MLA_PRO), sl] = norm_rope(kn[:, sl] + kr_sh, kg_ref[...]).astype(BF16)
            v_s[pl.ds(r0, MLA_PRO), sl] = jnp.where(lane == HEAD_DIM, 1.0, v[:, sl]).astype(BF16)
        return carry

    lax.fori_loop(0, seq // MLA_PRO, chunk, 0)

    row = lax.broadcasted_iota(jnp.int32, (tq, tq), 0)
    col = lax.broadcasted_iota(jnp.int32, (tq, tq), 1)
    for i in range(seq // tq):
        klen = (i + 1) * tq
        for h in range(N_HEADS):
            sl = slice(h * LANE, (h + 1) * LANE)
            qh = q_s[i * tq:(i + 1) * tq, sl]
            s = lax.dot_general(qh, k_s[0:klen, sl], NT, preferred_element_type=F32)
            diag = jnp.where(col <= row, s[:, klen - tq:], NEG)
            s = diag if i == 0 else jnp.concatenate([s[:, :klen - tq], diag], axis=1)
            p = jnp.exp(s - jnp.max(s, -1, keepdims=True)).astype(BF16)
            acc = jnp.dot(p, v_s[0:klen, sl], preferred_element_type=F32)
            out_h = acc[:, :HEAD_DIM] / acc[:, HEAD_DIM:HEAD_DIM + 1]
            o_ref[0, i * tq:(i + 1) * tq, h * HEAD_DIM:(h + 1) * HEAD_DIM] = out_h.astype(BF16)


def _mla(pa, cosv, sinv, cqg, ckvg, wuq, wk, wv, qg, kg, bsz, seq):
    hw = N_HEADS * LANE
    seq_blk = lambda w: pl.BlockSpec((1, seq, w), lambda b: (b, 0, 0))
    return pl.pallas_call(
        functools.partial(_mla_kernel, seq=seq),
        grid=(bsz,),
        in_specs=[seq_blk(PA_W), seq_blk(LANE), seq_blk(LANE),
                  _full((1, Q_RANK)), _full((1, KV_RANK)),
                  _full((Q_RANK, hw)), _full((KV_RANK, hw)), _full((KV_RANK, hw)),
                  _full((1, LANE)), _full((1, LANE))],
        out_specs=seq_blk(GROUP_W),
        out_shape=jax.ShapeDtypeStruct((bsz, seq, GROUP_W), BF16),
        scratch_shapes=[pltpu.VMEM((seq, hw), BF16)] * 3,
        compiler_params=_cparams(("parallel",)),
        name="mla",
    )(pa, cosv, sinv, cqg, ckvg, wuq, wk, wv, qg, kg)


ML_L = 256
ML_HALO = 8
ML_NB = 2


def _mlstm_kernel(pb_ref, pg_ref, cw_ref, cb_ref, wq_ref, wk_ref, gb_ref, hng_ref, tri_ref,
                  o_ref, xpad, st, mst, fst):
    for bi in range(ML_NB):
        _mlstm_chunk(pb_ref.at[bi], pg_ref.at[bi], cw_ref, cb_ref, wq_ref, wk_ref, gb_ref, hng_ref, tri_ref,
                     o_ref.at[bi], xpad.at[bi], st.at[bi], mst.at[bi], fst.at[bi])


def _mlstm_chunk(pb_ref, pg_ref, cw_ref, cb_ref, wq_ref, wk_ref, gb_ref, hng_ref, tri_ref,
                 o_ref, xpad, st, mst, fst):
    c = pl.program_id(1)
    L = ML_L

    @pl.when(c == 0)
    def _init():
        xpad[0:ML_HALO, :] = jnp.zeros((ML_HALO, GROUP_W), F32)
        st[...] = jnp.zeros(st.shape, F32)
        mst[...] = jnp.full(mst.shape, NEG, F32)
        fst[...] = jnp.zeros(fst.shape, F32)

    xqk = pb_ref[:, 0:GROUP_W].astype(F32)
    xpad[ML_HALO:ML_HALO + L, :] = xqk
    acc = cb_ref[...] + jnp.zeros((L, GROUP_W), F32)
    for j in range(CONV_B):
        off = ML_HALO - (CONV_B - 1) + j
        acc = acc + cw_ref[j:j + 1, :] * xpad[off:off + L, :]
    xpad[0:ML_HALO, :] = xqk[L - ML_HALO:, :]
    xc = (acc * _sigmoid(acc)).astype(BF16)
    q = jnp.dot(xc, wq_ref[...], preferred_element_type=F32)
    k = jnp.dot(xc, wk_ref[...], preferred_element_type=F32)

    gates = pg_ref[...] + gb_ref[...]
    ig = gates[:, :LANE]
    fg = gates[:, LANE:]
    lf = jnp.minimum(fg, 0.0) - jnp.log(1.0 + jnp.exp(-jnp.abs(fg)))
    fcol = _dot_exact_lhs(tri_ref[...], lf) + fst[...]
    fst[...] = fcol[L - 1:L, :]
    a_col = ig - fcol
    a_t = a_col.T

    rowi = lax.broadcasted_iota(jnp.int32, (L, L), 0)
    coli = lax.broadcasted_iota(jnp.int32, (L, L), 1)
    causal = coli <= rowi
    lane_v = lax.broadcasted_iota(jnp.int32, (L, LANE), 1)
    lane_w = lax.broadcasted_iota(jnp.int32, (L, GROUP_W), 1)

    mts, mns, mcs = [], [], []
    for h in range(N_HEADS):
        amat = jnp.where(causal, a_t[h:h + 1, :], NEG)
        mc = mst[:, h:h + 1]
        mt = jnp.maximum(jnp.max(amat, -1, keepdims=True), mc)
        mts.append((amat, mt))
        mcs.append(mc)
        mns.append(mt[L - 1:L, :])

    wg = jnp.exp(a_col[:, N_HEADS - 1:N_HEADS] - mns[N_HEADS - 1])
    for h in range(N_HEADS - 2, -1, -1):
        wg = jnp.where(lane_w < (h + 1) * HEAD_DIM, jnp.exp(a_col[:, h:h + 1] - mns[h]), wg)
    kw_t = (k * wg).T.astype(BF16)

    for h in range(N_HEADS):
        hs = slice(h * HEAD_DIM, (h + 1) * HEAD_DIM)
        amat, mt = mts[h]
        mc, mn = mcs[h], mns[h]
        p = jnp.exp(amat - mt)
        qh = q[:, hs].astype(BF16)
        kh = k[:, hs].astype(BF16)
        qk = lax.dot_general(qh, kh, NT, preferred_element_type=F32)
        w = (p * qk).astype(BF16)
        vext = jnp.where(lane_v == HEAD_DIM, 1.0,
                         pb_ref[:, GROUP_W + h * LANE:GROUP_W + (h + 1) * LANE].astype(F32)).astype(BF16)
        sth = st[h]
        nd = (jnp.dot(w, vext, preferred_element_type=F32)
              + jnp.exp(mc - mt) * jnp.dot(qh, sth.astype(BF16), preferred_element_type=F32))
        den = nd[:, HEAD_DIM:HEAD_DIM + 1]
        floor = jnp.exp(-(fcol[:, h:h + 1] + mt))
        hh = nd[:, :HEAD_DIM] / jnp.maximum(jnp.abs(den), floor)
        hh = hh * lax.rsqrt(jnp.mean(hh * hh, -1, keepdims=True) + EPS) * hng_ref[:, hs]
        xo = pb_ref[:, GROUP_W + N_HEADS * LANE + h * HEAD_DIM:
                    GROUP_W + N_HEADS * LANE + (h + 1) * HEAD_DIM].astype(F32)
        o_ref[:, hs] = (_sigmoid(xo) * hh).astype(BF16)
        st[h] = jnp.exp(mc - mn) * sth + jnp.dot(kw_t[hs, :], vext, preferred_element_type=F32)
        mst[:, h:h + 1] = mn


def _mlstm(pb, pg, cw, cb, wq, wk, gb, hng, tri, bsz, seq):
    L = ML_L
    return pl.pallas_call(
        _mlstm_kernel,
        grid=(bsz // ML_NB, seq // L),
        in_specs=[pl.BlockSpec((ML_NB, L, PB_W), lambda b, c: (b, c, 0)),
                  pl.BlockSpec((ML_NB, L, PG_W), lambda b, c: (b, c, 0)),
                  _full((CONV_B, GROUP_W)), _full((1, GROUP_W)),
                  _full((GROUP_W, GROUP_W)), _full((GROUP_W, GROUP_W)),
                  _full((1, PG_W)), _full((1, GROUP_W)), _full((L, L))],
        out_specs=pl.BlockSpec((ML_NB, L, GROUP_W), lambda b, c: (b, c, 0)),
        out_shape=jax.ShapeDtypeStruct((bsz, seq, GROUP_W), BF16),
        scratch_shapes=[pltpu.VMEM((ML_NB, L + ML_HALO, GROUP_W), F32),
                        pltpu.VMEM((ML_NB, N_HEADS, HEAD_DIM, LANE), F32),
                        pltpu.VMEM((ML_NB, 1, LANE), F32),
                        pltpu.VMEM((ML_NB, 1, LANE), F32)],
        compiler_params=_cparams(("parallel", "arbitrary")),
        name="mlstm",
    )(pb, pg, cw, cb, wq, wk, gb, hng, tri)


SP_STEP = 4


def _spatial_kernel(pc_ref, g_ref, b_ref, ws_ref, bias_ref, o_ref):
    T = SP_CHUNK
    rowi = lax.broadcasted_iota(jnp.int32, (T, N_HEADS * T), 0)
    coli = lax.broadcasted_iota(jnp.int32, (T, N_HEADS * T), 1)
    ws = jnp.where((coli & (T - 1)) <= rowi, ws_ref[...], 0.0).astype(BF16)
    rgrp = lax.broadcasted_iota(jnp.int32, (T, GROUP_W), 1) // HEAD_DIM
    for cidx in range(SP_STEP):
        rs = slice(cidx * T, (cidx + 1) * T)
        u = _gelu(pc_ref[rs, 0:GROUP_W].astype(F32))
        gv = _gelu(pc_ref[rs, GROUP_W:].astype(F32))
        mu = jnp.mean(gv, -1, keepdims=True)
        dv = gv - mu
        var = jnp.mean(dv * dv, -1, keepdims=True)
        vn = (dv * lax.rsqrt(var + EPS) * g_ref[...] + b_ref[...]).astype(BF16)
        zero = jnp.zeros_like(vn)
        vbig = jnp.concatenate([jnp.where(rgrp == g, vn, zero) for g in range(N_HEADS)], axis=0)
        sg = jnp.dot(ws, vbig, preferred_element_type=F32) + bias_ref[...]
        o_ref[rs, :] = (u * sg).astype(BF16)


def _spatial(pc, g, b, ws_cat, bias_full):
    t = pc.shape[0]
    tm = SP_STEP * SP_CHUNK
    return pl.pallas_call(
        _spatial_kernel,
        grid=(t // tm,),
        in_specs=[pl.BlockSpec((tm, PC_W), lambda i: (i, 0)),
                  _full((1, GROUP_W)), _full((1, GROUP_W)),
                  _full((SP_CHUNK, N_HEADS * SP_CHUNK)), _full((SP_CHUNK, GROUP_W))],
        out_specs=pl.BlockSpec((tm, GROUP_W), lambda i: (i, 0)),
        out_shape=jax.ShapeDtypeStruct((t, GROUP_W), BF16),
        compiler_params=_cparams(("parallel",)),
        name="spatial",
    )(pc, g, b, ws_cat, bias_full)


CV_T = 256
CV_HALO = 32


def _convmod_kernel(pd_ref, w_ref, b_ref, avg_ref, g_ref, be_ref, o_ref, ypad, zsh):
    i = pl.program_id(1)
    T = CV_T

    @pl.when(i == 0)
    def _init():
        ypad[0:CV_HALO, :] = jnp.zeros((CV_HALO, GROUP_W), F32)

    a = pd_ref[0, :, 0:GROUP_W].astype(F32)
    b = pd_ref[0, :, GROUP_W:].astype(F32)
    y = a * _sigmoid(b)
    ypad[CV_HALO:CV_HALO + T, :] = y
    acc = b_ref[...] + jnp.zeros((T, GROUP_W), F32)
    off0 = CV_HALO - (CONV_D - 1)
    for b in range(SUBLANES):
        offs = [o for o in range(off0, off0 + CONV_D) if o % SUBLANES == b]
        span = offs[-1] - b + T
        src = ypad
        if b:
            zsh[0:span, :] = ypad[b:b + span, :]
            src = zsh
        for o in offs:
            acc = acc + w_ref[o - off0:o - off0 + 1, :] * src[o - b:o - b + T, :]
    ypad[0:CV_HALO, :] = y[T - CV_HALO:, :]
    mu = _dot_exact_rhs(acc, avg_ref[...])
    dv = acc - mu
    var = _dot_exact_rhs(dv * dv, avg_ref[...])
    yn = dv * lax.rsqrt(var + EPS) * g_ref[...] + be_ref[...]
    o_ref[0] = (yn * _sigmoid(yn)).astype(BF16)


def _convmod(pd, w, b, avg, g, be, bsz, seq):
    return pl.pallas_call(
        _convmod_kernel,
        grid=(bsz, seq // CV_T),
        in_specs=[pl.BlockSpec((1, CV_T, PD_W), lambda bb, i: (bb, i, 0)),
                  _full((CONV_D, GROUP_W)), _full((1, GROUP_W)), _full((GROUP_W, GROUP_W)),
                  _full((1, GROUP_W)), _full((1, GROUP_W))],
        out_specs=pl.BlockSpec((1, CV_T, GROUP_W), lambda bb, i: (bb, i, 0)),
        out_shape=jax.ShapeDtypeStruct((bsz, seq, GROUP_W), BF16),
        scratch_shapes=[pltpu.VMEM((CV_T + CV_HALO, GROUP_W), F32)] * 2,
        compiler_params=_cparams(("parallel", "arbitrary")),
        name="convmod",
    )(pd, w, b, avg, g, be)


OP_TM = 512
PQ_W = P_HEADS * P_HALF
KROWS = P_KEYS * P_HEADS


def _outproj_kernel(x_ref, ma_ref, mb_ref, mc_ref, md_ref, wo_ref, g_ref, wpq_ref, k1_ref, k2_ref,
                    h_ref, xnt_ref, s1_ref, s2_ref):
    acc = x_ref[...]
    for g, m in enumerate((ma_ref, mb_ref, mc_ref, md_ref)):
        acc = acc + jnp.dot(m[...], wo_ref[g * GROUP_W:(g + 1) * GROUP_W, :], preferred_element_type=F32)
    h_ref[...] = acc
    hn = acc * lax.rsqrt(jnp.mean(acc * acc, -1, keepdims=True) + EPS) * g_ref[...]
    xnt_ref[...] = hn.T.astype(BF16)
    q = jnp.dot(hn.astype(BF16), wpq_ref[...], preferred_element_type=F32).astype(BF16)
    s1_ref[...] = lax.dot_general(k1_ref[...], q[:, :PQ_W], NT, preferred_element_type=F32)
    s2_ref[...] = lax.dot_general(k2_ref[...], q[:, PQ_W:], NT, preferred_element_type=F32)


def _outproj(x, ma, mb, mc, md, wo, g, wpq, k1, k2):
    t = x.shape[0]
    tm = OP_TM
    mix = pl.BlockSpec((tm, GROUP_W), lambda i: (i, 0))
    tcol = pl.BlockSpec((KROWS, tm), lambda i: (0, i))
    return pl.pallas_call(
        _outproj_kernel,
        grid=(t // tm,),
        in_specs=[pl.BlockSpec((tm, D_MODEL), lambda i: (i, 0)), mix, mix, mix, mix,
                  _full((D_MODEL, D_MODEL)), _full((1, D_MODEL)), _full((D_MODEL, 2 * PQ_W)),
                  _full((KROWS, PQ_W)), _full((KROWS, PQ_W))],
        out_specs=[pl.BlockSpec((tm, D_MODEL), lambda i: (i, 0)),
                   pl.BlockSpec((D_MODEL, tm), lambda i: (0, i)), tcol, tcol],
        out_shape=[jax.ShapeDtypeStruct((t, D_MODEL), F32),
                   jax.ShapeDtypeStruct((D_MODEL, t), BF16),
                   jax.ShapeDtypeStruct((KROWS, t), F32),
                   jax.ShapeDtypeStruct((KROWS, t), F32)],
        compiler_params=_cparams(("parallel",)),
        name="outproj",
    )(x, ma, mb, mc, md, wo, g, wpq, k1, k2)


RT_T = LANE
BF_ROWS = 16


def _bitonic_merge(v):
    n = len(v)
    if n == 1:
        return v
    half = n // 2
    hi = [jnp.maximum(v[i], v[i + half]) for i in range(half)]
    lo = [jnp.minimum(v[i], v[i + half]) for i in range(half)]
    return _bitonic_merge(hi) + _bitonic_merge(lo)


def _bitonic_sort(v):
    n = len(v)
    if n == 1:
        return v
    return _bitonic_merge(_bitonic_sort(v[:n // 2]) + _bitonic_sort(v[n // 2:])[::-1])


def _merge_top(a, b):
    n = len(a)
    c = list(a)
    for k, bv in enumerate(b):
        c[n - 1 - k] = jnp.maximum(a[n - 1 - k], bv)
    return _bitonic_merge(c)


def _sorted_top(ref, lanes):
    top = None
    for grp in range(P_KEYS // P_TOPK):
        tiles = [ref[(grp * P_TOPK + k) * P_HEADS:(grp * P_TOPK + k + 1) * P_HEADS, lanes]
                 for k in range(P_TOPK)]
        srt = _bitonic_sort(tiles)
        top = srt if top is None else _merge_top(top, srt)
    return top


def _route_kernel(s1_ref, s2_ref, cnt_ref, w_ref, rk_ref, p2h_ref):
    for g in range(RT_T // LANE):
        lanes = slice(g * LANE, (g + 1) * LANE)
        t1 = _sorted_top(s1_ref, lanes)
        t2 = _sorted_top(s2_ref, lanes)
        rows = [[t1[a] + t2[b] for b in range(P_TOPK // (a + 1))] for a in range(P_TOPK)]
        top = rows[0]
        for a in range(1, P_TOPK):
            top = _merge_top(top, rows[a])
        tau = top[P_TOPK - 1]
        cmax = rows[0][0]
        z = jnp.zeros((P_HEADS, LANE), F32)
        cnt_a = []
        for a in range(P_TOPK):
            ca = jnp.zeros((P_HEADS, LANE), F32)
            for cnd in rows[a]:
                sel = cnd >= tau
                z = z + jnp.where(sel, jnp.exp(cnd - cmax), 0.0)
                ca = ca + jnp.where(sel, 1.0, 0.0)
            cnt_a.append(ca)
        zinv = 1.0 / z

        def body(n, carry):
            r0 = pl.multiple_of(n * P_HEADS, P_HEADS)
            s1 = s1_ref[pl.ds(r0, P_HEADS), lanes]
            cn = jnp.where(s1 + t2[0] >= tau, 1.0, 0.0)
            for a in range(P_TOPK // 2 - 1, -1, -1):
                cn = jnp.where(s1 == t1[a], cnt_a[a], cn)
            cnt_ref[pl.ds(r0, P_HEADS), lanes] = cn
            w_ref[pl.ds(r0, P_HEADS), lanes] = jnp.exp(s1 - t1[0]) * zinv
            return carry

        lax.fori_loop(0, P_KEYS, body, 0, unroll=16)

        for h in range(P_HEADS):
            tb = [jnp.broadcast_to(t2[b][h:h + 1, :], (BF_ROWS, LANE)) for b in range(P_TOPK)]

            def rbody(r, carry, h=h, tb=tb):
                r0 = pl.multiple_of(h * P_KEYS + r * BF_ROWS, BF_ROWS)
                s2 = s2_ref[pl.ds(h + r * (BF_ROWS * P_HEADS), BF_ROWS, stride=P_HEADS), lanes]
                rk = jnp.full((BF_ROWS, LANE), float(P_TOPK), F32)
                for b in range(P_TOPK - 1, -1, -1):
                    rk = jnp.where(tb[b] > s2, rk, float(b))
                rk_ref[pl.ds(r0, BF_ROWS), lanes] = rk.astype(BF16)
                p2h_ref[pl.ds(r0, BF_ROWS), lanes] = jnp.exp(s2 - tb[0]).astype(BF16)
                return carry

            lax.fori_loop(0, P_KEYS // BF_ROWS, rbody, 0, unroll=True)


def _route(s1, s2):
    t = s1.shape[1]
    blk = pl.BlockSpec((KROWS, RT_T), lambda i: (0, i))
    return pl.pallas_call(
        _route_kernel,
        grid=(t // RT_T,),
        in_specs=[blk, blk],
        out_specs=[blk, blk, blk, blk],
        out_shape=[jax.ShapeDtypeStruct((KROWS, t), F32), jax.ShapeDtypeStruct((KROWS, t), F32),
                   jax.ShapeDtypeStruct((KROWS, t), BF16), jax.ShapeDtypeStruct((KROWS, t), BF16)],
        compiler_params=_cparams(("parallel",)),
        name="route",
    )(s1, s2)


PE_TT = 512
PE_EB = 1024
PE_K1 = PE_EB // P_KEYS


PE_GL = 256


def _peer_kernel(xnt_ref, u_ref, vtp_ref, vtl_ref, cnt_ref, w_ref, rk_ref, p2h_ref, h_ref, gc_ref, o_ref,
                 acc, ht, gsc, cnt_s, w_s, rk_s, p2_s):
    j = pl.program_id(1)
    slot = j % 2

    @pl.when(j == 0)
    def _init():
        acc[...] = jnp.zeros(acc.shape, F32)
        ht[1] = jnp.zeros(ht.shape[1:], BF16)
        rk_s[...] = rk_ref[...]
        p2_s[...] = p2h_ref[...]

    e0 = j * PE_K1
    cnt_s[...] = cnt_ref[pl.ds(e0, PE_K1)]
    w_s[...] = w_ref[pl.ds(e0, PE_K1)]

    ngrp = P_KEYS // BF_ROWS
    bzero = jnp.zeros((), BF16)
    gtile = lambda r: jnp.concatenate(
        [jnp.broadcast_to(gc_ref[r:r + 1, :], (BF_ROWS, PE_GL)).astype(BF16)] * (PE_EB // BF_ROWS), axis=0)
    gc1, gc2 = gtile(0), gtile(1)
    for s in range(PE_TT // PE_GL):
        ls = slice(s * PE_GL, (s + 1) * PE_GL)
        acc[:, ls] += jnp.dot(vtp_ref[...], ht[1 - slot, :, ls], preferred_element_type=F32)
        for c in range(PE_K1):
            gate = [jnp.zeros((BF_ROWS, PE_GL), BF16) for _ in range(ngrp)]
            for h in range(P_HEADS):
                cn = jnp.broadcast_to(cnt_s[c, h:h + 1, ls], (BF_ROWS, PE_GL)).astype(BF16)
                wg = jnp.broadcast_to(w_s[c, h:h + 1, ls], (BF_ROWS, PE_GL)).astype(BF16)
                for r in range(ngrp):
                    rows = slice(h * P_KEYS + r * BF_ROWS, h * P_KEYS + (r + 1) * BF_ROWS)
                    sel = jnp.where(rk_s[rows, ls] < cn, p2_s[rows, ls], bzero)
                    gate[r] = gate[r] + sel * wg
            for r in range(ngrp):
                rows = slice(c * P_KEYS + r * BF_ROWS, c * P_KEYS + (r + 1) * BF_ROWS)
                gsc[rows, ls] = gate[r]
        a = jnp.dot(u_ref[...], xnt_ref[:, ls], preferred_element_type=F32)
        ht[slot, :, ls] = _gelu(a.astype(BF16), gc1, gc2) * gsc[:, ls]

    @pl.when(j == pl.num_programs(1) - 1)
    def _fin():
        last = jnp.dot(vtl_ref[...], ht[slot], preferred_element_type=F32)
        o_ref[...] = h_ref[...] + (acc[...] + last).T


def _peer(xnt, u_all, vt_all, layer, cnt, w, rk, p2h, hres):
    t = xnt.shape[1]
    nj = N_EXPERTS // PE_EB
    gconst = jnp.zeros((SUBLANES, PE_GL), F32).at[0].set(GELU_C1).at[1].set(GELU_C2)
    tok = lambda r: pl.BlockSpec((r, PE_TT), lambda i, j: (0, i))
    tok3 = pl.BlockSpec((P_KEYS, P_HEADS, PE_TT), lambda i, j: (0, 0, i))
    return pl.pallas_call(
        _peer_kernel,
        grid=(t // PE_TT, nj),
        in_specs=[tok(D_MODEL),
                  pl.BlockSpec((None, PE_EB, D_MODEL), lambda i, j: (layer, j, 0)),
                  pl.BlockSpec((None, D_MODEL, PE_EB), lambda i, j: (layer, 0, jnp.maximum(j - 1, 0))),
                  pl.BlockSpec((None, D_MODEL, PE_EB), lambda i, j: (layer, 0, nj - 1)),
                  tok3, tok3, tok(KROWS), tok(KROWS),
                  pl.BlockSpec((PE_TT, D_MODEL), lambda i, j: (i, 0)),
                  _full((SUBLANES, PE_GL))],
        out_specs=pl.BlockSpec((PE_TT, D_MODEL), lambda i, j: (i, 0)),
        out_shape=jax.ShapeDtypeStruct((t, D_MODEL), F32),
        scratch_shapes=[pltpu.VMEM((D_MODEL, PE_TT), F32), pltpu.VMEM((2, PE_EB, PE_TT), BF16),
                        pltpu.VMEM((PE_EB, PE_TT), BF16),
                        pltpu.VMEM((PE_K1, P_HEADS, PE_TT), F32), pltpu.VMEM((PE_K1, P_HEADS, PE_TT), F32),
                        pltpu.VMEM((KROWS, PE_TT), BF16), pltpu.VMEM((KROWS, PE_TT), BF16)],
        compiler_params=_cparams(("parallel", "arbitrary")),
        name="peer",
    )(xnt, u_all, vt_all, vt_all, cnt, w, rk, p2h, hres, gconst)


def _pad_cols(w, width):
    return jnp.pad(w, ((0, 0), (0, width - w.shape[1])))


def _prep_layer(l, p):
    w_in = p["w_in"][l]
    o = 0
    cuts = {}
    for name, wdt in (("a", Q_RANK + KV_RANK + ROPE), ("bqk", GROUP_W), ("bv", GROUP_W), ("bo", GROUP_W),
                      ("bi", N_HEADS), ("bf", N_HEADS), ("c", 2 * GROUP_W), ("d", 2 * GROUP_W)):
        cuts[name] = w_in[:, o:o + wdt]
        o += wdt
    bv = cuts["bv"].reshape(D_MODEL, N_HEADS, HEAD_DIM)
    bv = jnp.pad(bv, ((0, 0), (0, 0), (0, LANE - HEAD_DIM))).reshape(D_MODEL, N_HEADS * LANE)
    w_all = jnp.concatenate([
        _pad_cols(cuts["a"], PA_W), cuts["bqk"], bv, cuts["bo"],
        _pad_cols(cuts["bi"], LANE), _pad_cols(cuts["bf"], LANE), cuts["c"], cuts["d"]], axis=1).astype(BF16)

    def head_pad(w, width):
        return jnp.pad(w, ((0, 0), (0, 0), (0, LANE - width))).reshape(w.shape[0], N_HEADS * LANE)

    wuq = head_pad(p["a_w_uq"][l].reshape(Q_RANK, N_HEADS, QK_DIM), QK_DIM).astype(BF16)
    wukv = p["a_w_ukv"][l].reshape(KV_RANK, N_HEADS, NOPE + HEAD_DIM)
    wk = head_pad(wukv[:, :, :NOPE], NOPE).astype(BF16)
    wv = head_pad(wukv[:, :, NOPE:], HEAD_DIM).astype(BF16)
    qg = _pad_cols(p["a_qn_g"][l][None, :] * (QK_DIM ** -0.5), LANE)
    kg = _pad_cols(p["a_kn_g"][l][None, :], LANE)

    eye = jnp.eye(N_HEADS, dtype=F32)
    wq_bd = jnp.einsum("hde,hg->hdge", p["b_w_q"][l], eye).reshape(GROUP_W, GROUP_W).astype(BF16)
    wk_bd = (jnp.einsum("hde,hg->hdge", p["b_w_k"][l], eye).reshape(GROUP_W, GROUP_W)
             * (HEAD_DIM ** -0.5)).astype(BF16)
    gb = jnp.concatenate([_pad_cols(p["b_b_i"][l][None, :], LANE), _pad_cols(p["b_b_f"][l][None, :], LANE)], axis=1)

    ws_cat = p["c_w_s"][l].transpose(1, 0, 2).reshape(SP_CHUNK, N_HEADS * SP_CHUNK)
    bias_full = jnp.repeat(p["c_b_s"][l].T, HEAD_DIM, axis=1)

    wpq = p["p_w_q"][l].reshape(D_MODEL, P_HEADS, 2, P_HALF).transpose(0, 2, 1, 3).reshape(D_MODEL, 2 * PQ_W)
    keys = p["p_sub_keys"][l]
    eye8 = jnp.eye(P_HEADS, dtype=F32)
    kmat = [jnp.einsum("hnd,hg->nhgd", keys[:, s], eye8).reshape(KROWS, PQ_W).astype(BF16) for s in range(2)]
    return dict(
        norm1_g=p["norm1_g"][l][None, :], w_all=w_all,
        cqg=p["a_cq_g"][l][None, :], ckvg=p["a_ckv_g"][l][None, :], wuq=wuq, wk=wk, wv=wv, qg=qg, kg=kg,
        b_cw=p["b_conv_w"][l], b_cb=p["b_conv_b"][l][None, :], wq_bd=wq_bd, wk_bd=wk_bd, gb=gb,
        hng=p["b_hn_g"][l].reshape(1, GROUP_W),
        c_g=p["c_ln_g"][l][None, :], c_b=p["c_ln_b"][l][None, :], ws_cat=ws_cat, bias_full=bias_full,
        d_w=p["d_dw_w"][l], d_b=p["d_dw_b"][l][None, :],
        d_g=p["d_cn_g"][l].reshape(1, GROUP_W), d_be=p["d_cn_b"][l].reshape(1, GROUP_W),
        w_out=p["w_out"][l].astype(BF16), norm2_g=p["norm2_g"][l][None, :], wpq=wpq.astype(BF16),
        k1=kmat[0], k2=kmat[1],
    )


def _layer(x, rope, consts, w, tables, layer, bsz, seq):
    t = bsz * seq
    pa, pb, pg, pc, pd = _inproj(x, w["norm1_g"], w["w_all"])
    r3 = lambda a: a.reshape(bsz, seq, a.shape[-1])
    mix_a = _mla(r3(pa), r3(rope[0]), r3(rope[1]), w["cqg"], w["ckvg"], w["wuq"], w["wk"], w["wv"],
                 w["qg"], w["kg"], bsz, seq).reshape(t, GROUP_W)
    mix_b = _mlstm(r3(pb), r3(pg), w["b_cw"], w["b_cb"], w["wq_bd"], w["wk_bd"], w["gb"], w["hng"],
                   consts["tri"], bsz, seq).reshape(t, GROUP_W)
    mix_c = _spatial(pc, w["c_g"], w["c_b"], w["ws_cat"], w["bias_full"])
    mix_d = _convmod(r3(pd), w["d_w"], w["d_b"], consts["avg"], w["d_g"], w["d_be"], bsz, seq).reshape(t, GROUP_W)
    h, xnt, s1, s2 = _outproj(x, mix_a, mix_b, mix_c, mix_d, w["w_out"], w["norm2_g"], w["wpq"],
                              w["k1"], w["k2"])
    cnt, wgt, rk, p2h = _route(s1, s2)
    shp = (P_KEYS, P_HEADS, t)
    return _peer(xnt, tables[0], tables[1], layer, cnt.reshape(shp), wgt.reshape(shp), rk, p2h, h)


def _consts():
    half = ROPE // 2
    inv_freq = ROPE_BASE ** (-jnp.arange(0, ROPE, 2, dtype=F32) / ROPE)
    invf = jnp.zeros((1, LANE), F32)
    invf = invf.at[0, NOPE:NOPE + half].set(inv_freq).at[0, NOPE + half:QK_DIM].set(inv_freq)
    tri = jnp.tril(jnp.ones((ML_L, ML_L), F32)).astype(BF16)
    grp = jnp.arange(GROUP_W) // HEAD_DIM
    avg = ((grp[:, None] == grp[None, :]).astype(F32) / HEAD_DIM).astype(BF16)
    return dict(invf=invf, tri=tri, avg=avg)


def kernel(x, positions, norm1_g, w_in, a_cq_g, a_ckv_g, a_w_uq, a_w_ukv, a_qn_g, a_kn_g, b_conv_w, b_conv_b, b_w_q, b_w_k, b_b_i, b_b_f, b_hn_g, c_ln_g, c_ln_b, c_w_s, c_b_s, d_dw_w, d_dw_b, d_cn_g, d_cn_b, w_out, norm2_g, p_w_q, p_sub_keys, p_u, p_v):
    params = dict(norm1_g=norm1_g, w_in=w_in, a_cq_g=a_cq_g, a_ckv_g=a_ckv_g, a_w_uq=a_w_uq, a_w_ukv=a_w_ukv,
                  a_qn_g=a_qn_g, a_kn_g=a_kn_g, b_conv_w=b_conv_w, b_conv_b=b_conv_b, b_w_q=b_w_q, b_w_k=b_w_k,
                  b_b_i=b_b_i, b_b_f=b_b_f, b_hn_g=b_hn_g, c_ln_g=c_ln_g, c_ln_b=c_ln_b, c_w_s=c_w_s,
                  c_b_s=c_b_s, d_dw_w=d_dw_w, d_dw_b=d_dw_b, d_cn_g=d_cn_g, d_cn_b=d_cn_b, w_out=w_out,
                  norm2_g=norm2_g, p_w_q=p_w_q, p_sub_keys=p_sub_keys, p_u=p_u, p_v=p_v)
    bsz, seq, d = x.shape
    consts = _consts()
    rope = _rope_tables(positions.reshape(bsz * seq, 1), consts["invf"])
    xf = x.reshape(bsz * seq, d)
    tables = (p_u.astype(BF16), p_v.astype(BF16).transpose(0, 2, 1))
    for l in range(norm1_g.shape[0]):
        xf = _layer(xf, rope, consts, _prep_layer(l, params), tables, l, bsz, seq)
    return xf.reshape(bsz, seq, d)
```
